```python
import math
import jax
import jax.numpy as jnp
from jax import lax
import numpy as np

D_MODEL = 1024
BATCH = 32
SEQ = 256
DEPTH = 2
DEC_BATCH = 8
DEC_SEQ = 2048
PAST_LEN = 512

GRID_W = 64
N_EVEN = (DEPTH + 1) // 2
N_ODD = DEPTH // 2
RWKV_HEADS = 12
HEAD_SIZE = 64
D_RWKV = RWKV_HEADS * HEAD_SIZE
DECAY_LORA = 64
AAA_LORA = 64
GATE_LORA = 128
GN_EPS = 64e-5
FNET_GROUPS = 4
FNET_GROUP_W = 64
D_FNET = FNET_GROUPS * FNET_GROUP_W
D_SHIFTED = 3 * D_RWKV + 2 * DECAY_LORA + 2 * AAA_LORA + GATE_LORA
D_AB_IN = D_SHIFTED + D_FNET
HY_BANDS = 16
HY_EMB = 2 * HY_BANDS + 1
HY_ORDER = 64
PEER_HEADS = 8
N_KEYS = 128
N_EXPERTS = N_KEYS * N_KEYS
PEER_DK = 256
PEER_TOPK = 16
PEER_BLOCK = 128
LN_EPS = 1e-5
DN_ALPHA = (2 * DEPTH) ** 0.25
DN_BETA = (8 * DEPTH) ** -0.25

kernel_name = 'hybrid_rwkv7_fnet_hyena_peer_prefix_step'

F32 = jnp.float32


def layer_norm(x, g=None, b=None):
    xf = x.astype(F32)
    mu = jnp.mean(xf, axis=-1, keepdims=True)
    var = jnp.mean(jnp.square(xf - mu), axis=-1, keepdims=True)
    y = (xf - mu) * lax.rsqrt(var + LN_EPS)
    if g is not None:
        y = y * g.astype(F32) + b.astype(F32)
    return y.astype(x.dtype)


def shift_prev(z):
    return jnp.pad(z, ((0, 0), (1, 0), (0, 0)))[:, :-1]


def shift_next(z):
    return jnp.pad(z, ((0, 0), (0, 1), (0, 0)))[:, 1:]


def modulation(cvec, w_mod, b_mod):
    m = jax.nn.silu(cvec) @ w_mod + b_mod
    return jnp.split(m, 6, axis=-1)


def grid_pos_embed(n_tokens):
    rows = n_tokens // GRID_W
    row = jnp.repeat(jnp.arange(rows, dtype=F32), GRID_W)
    col = jnp.tile(jnp.arange(GRID_W, dtype=F32), rows)
    quarter = D_MODEL // 4
    omega = 1.0 / (10000.0 ** (jnp.arange(quarter, dtype=F32) / quarter))
    ra = row[:, None] * omega[None, :]
    ca = col[:, None] * omega[None, :]
    return jnp.concatenate([jnp.sin(ra), jnp.cos(ra), jnp.sin(ca), jnp.cos(ca)], axis=-1)


def wkv7_scan(r, decay, k, v, kk, a, s0, reverse):
    def step(s, inp):
        r_t, w_t, k_t, v_t, kk_t, a_t = inp
        sa = jnp.einsum('bhij,bhj->bhi', s, -kk_t)
        s = (s * w_t[:, :, None, :]
             + sa[..., None] * (kk_t * a_t)[:, :, None, :]
             + v_t[..., None] * k_t[:, :, None, :])
        return s, jnp.einsum('bhij,bhj->bhi', s, r_t)
    xs = tuple(jnp.swapaxes(t, 0, 1) for t in (r, decay, k, v, kk, a))
    s_fin, ys = lax.scan(step, s0, xs, reverse=reverse)
    return jnp.swapaxes(ys, 0, 1), s_fin


def rwkv_mix(zs, e, P, s0_f, s0_b):
    B, L, _ = zs.shape
    splits = [D_RWKV, 2 * D_RWKV, 3 * D_RWKV,
              3 * D_RWKV + DECAY_LORA, 3 * D_RWKV + 2 * DECAY_LORA,
              3 * D_RWKV + 2 * DECAY_LORA + AAA_LORA, 3 * D_RWKV + 2 * DECAY_LORA + 2 * AAA_LORA]
    r, k, v, zw_f, zw_b, za_f, za_b, zg = jnp.split(zs, splits, axis=-1)

    def heads(t):
        return t.astype(F32).reshape(B, L, RWKV_HEADS, HEAD_SIZE)

    r_h, k_h, v_h = heads(r), heads(k), heads(v)
    kk = heads(k * P['rwkv_k_k'][e])
    kk = kk * lax.rsqrt(jnp.sum(kk * kk, axis=-1, keepdims=True) + 1e-12)
    k_a = P['rwkv_k_a'][e].astype(F32).reshape(RWKV_HEADS, HEAD_SIZE)
    ys, finals, k_dirs = [], [], []
    for d, (zw, za, s0, rev) in enumerate(((zw_f, za_f, s0_f, False), (zw_b, za_b, s0_b, True))):
        w_raw = (P['rwkv_w0'][e, d] + jnp.tanh(zw) @ P['rwkv_w2'][e, d]).astype(F32)
        decay = heads(jnp.exp(-jnp.exp(-jax.nn.softplus(-w_raw) - 0.5)))
        a = heads(jax.nn.sigmoid((P['rwkv_a0'][e, d] + za @ P['rwkv_a2'][e, d]).astype(F32)))
        k_d = k_h * (1.0 + (a - 1.0) * k_a)
        y_d, s_d = wkv7_scan(r_h, decay, k_d, v_h, kk, a, s0.astype(F32), rev)
        ys.append(y_d)
        finals.append(s_d)
        k_dirs.append(k_d)
    y = ys[0] + ys[1]
    mu = jnp.mean(y, axis=-1, keepdims=True)
    var = jnp.mean(jnp.square(y - mu), axis=-1, keepdims=True)
    y = ((y - mu) * lax.rsqrt(var + GN_EPS)).reshape(B, L, D_RWKV)
    y = y * P['rwkv_gn_g'][e].astype(F32) + P['rwkv_gn_b'][e].astype(F32)
    k_bonus = 0.5 * (k_dirs[0] + k_dirs[1])
    bonus = jnp.sum(r_h * k_bonus * P['rwkv_r_k'][e].astype(F32), axis=-1, keepdims=True) * v_h
    y = y + bonus.reshape(B, L, D_RWKV)
    g = jax.nn.sigmoid(zg) @ P['rwkv_g2'][e]
    return y.astype(zs.dtype) * g, finals[0], finals[1]


def fourier_mix(zf):
    B, L, _ = zf.shape
    z = zf.astype(F32).reshape(B, L, FNET_GROUPS, FNET_GROUP_W)
    y = jnp.fft.fft2(z, axes=(1, 3), norm='ortho').real
    return y.reshape(B, L, D_FNET).astype(zf.dtype)


def ab_mixer(h, e, P, s0_f, s0_b):
    z = h @ P['ab_w_in'][e]
    zs, zf = z[..., :D_SHIFTED], z[..., D_SHIFTED:]
    zs = zs + P['ab_mu'][e] * (0.5 * (shift_prev(zs) + shift_next(zs)) - zs)
    y_a, s_f, s_b = rwkv_mix(zs, e, P, s0_f, s0_b)
    y_b = fourier_mix(zf)
    y = jnp.concatenate([y_a, y_b], axis=-1) @ P['ab_w_out'][e]
    return y, s_f, s_b


def hyena_filters(L, w1, b1, w2, b2, w3, freq, decay):
    pos = jnp.arange(L, dtype=F32)[:, None]
    t = pos / (L - 1)
    bands = jnp.linspace(1e-4, HY_BANDS - 1, HY_BANDS, dtype=F32)[None, :]
    ang = 2.0 * math.pi * pos / L * bands
    feats = jnp.concatenate([t, jnp.cos(ang), -jnp.sin(ang)], axis=-1)
    fr = freq.astype(F32)
    hid = jnp.sin(fr * (feats @ w1.astype(F32) + b1.astype(F32)))
    hid = jnp.sin(fr * (hid @ w2.astype(F32) + b2.astype(F32)))
    filt = (hid @ w3.astype(F32)) * jnp.exp(-t * jnp.abs(decay.astype(F32)))
    return filt[:, :D_MODEL], filt[:, D_MODEL:]


def long_conv(u, h_f, h_b, skip):
    L, D = h_f.shape
    taps = jnp.concatenate([h_f, jnp.zeros((1, D), F32), h_b[:0:-1]], axis=0)
    uf = jnp.fft.rfft(u, n=2 * L, axis=1)
    tf = jnp.fft.rfft(taps, axis=0)
    y = jnp.fft.irfft(uf * tf[None], n=2 * L, axis=1)[:, :L]
    return y + u * skip


def hyena_mixer(h, o, P):
    L = h.shape[1]
    z = h @ P['hy_w_in'][o]
    cw = P['hy_conv_w'][o]
    z = cw[0] * shift_prev(z) + cw[1] * z + cw[2] * shift_next(z) + P['hy_conv_b'][o]
    x0, x1, v = jnp.split(z, 3, axis=-1)
    h_f, h_b = hyena_filters(L, P['hy_filt_w1'][o], P['hy_filt_b1'][o], P['hy_filt_w2'][o],
                             P['hy_filt_b2'][o], P['hy_filt_w3'][o], P['hy_filt_freq'][o],
                             P['hy_filt_decay'][o])
    y = long_conv((v * x1).astype(F32), h_f, h_b, P['hy_skip'][o].astype(F32))
    return (y.astype(h.dtype) * x0) @ P['hy_w_out'][o]


def peer(h, layer, P):
    B, L, D = h.shape
    T = B * L
    x = h.reshape(T, D)
    q = (x @ P['peer_wq'][layer]).astype(F32).reshape(T, PEER_HEADS, 2, PEER_DK // 2)
    s1 = jnp.einsum('thd,hnd->thn', q[:, :, 0], P['peer_k1'][layer].astype(F32))
    s2 = jnp.einsum('thd,hnd->thn', q[:, :, 1], P['peer_k2'][layer].astype(F32))
    v1, i1 = lax.top_k(s1, PEER_TOPK)
    v2, i2 = lax.top_k(s2, PEER_TOPK)
    n_cand = PEER_TOPK * PEER_TOPK
    cand = (v1[..., :, None] + v2[..., None, :]).reshape(T, PEER_HEADS, n_cand)
    cidx = (i1[..., :, None] * N_KEYS + i2[..., None, :]).reshape(T, PEER_HEADS, n_cand)
    best, pos = lax.top_k(cand, PEER_TOPK)
    idx = jnp.take_along_axis(cidx, pos, axis=-1)
    gates = jax.nn.softmax(best, axis=-1).astype(h.dtype)
    n_blk = T // PEER_BLOCK
    n_sel = PEER_HEADS * PEER_TOPK
    u_tab = P['peer_u'][layer]
    v_tab = P['peer_v'][layer]

    def expert_block(args):
        xb, ib, gb = args
        act = jax.nn.gelu(jnp.einsum('td,ted->te', xb, u_tab[ib])) * gb
        return jnp.einsum('te,ted->td', act, v_tab[ib])

    out = lax.map(expert_block, (x.reshape(n_blk, PEER_BLOCK, D),
                                 idx.reshape(n_blk, PEER_BLOCK, n_sel),
                                 gates.reshape(n_blk, PEER_BLOCK, n_sel)))
    return out.reshape(B, L, D)


def run_layer(layer, x, cvec, s0_f, s0_b, P):
    sh1, sc1, g1, sh2, sc2, g2 = modulation(cvec, P['w_mod'][layer], P['b_mod'][layer])
    h = layer_norm(x) * (1.0 + sc1) + sh1
    if layer % 2 == 0:
        y, s_f, s_b = ab_mixer(h, layer // 2, P, s0_f, s0_b)
    else:
        y = hyena_mixer(h, layer // 2, P)
        s_f, s_b = None, None
    x = layer_norm(DN_ALPHA * x + g1 * y, P['ln_mix_g'][layer], P['ln_mix_b'][layer])
    h = layer_norm(x) * (1.0 + sc2) + sh2
    y = peer(h, layer, P)
    x = layer_norm(DN_ALPHA * x + g2 * y, P['ln_ffn_g'][layer], P['ln_ffn_b'][layer])
    return x, s_f, s_b


def setup_inputs(seed: int = 0) -> dict:
    key = jax.random.key(seed)
    keys = iter(jax.random.split(key, 48))

    def nrm(shape, std):
        return std * jax.random.normal(next(keys), shape, F32)

    D = D_MODEL
    st_shape = (DEC_BATCH, N_EVEN, RWKV_HEADS, HEAD_SIZE, HEAD_SIZE)
    return {
        'x_prompt': nrm((BATCH, SEQ, D), 1.0),
        'x_sample': nrm((DEC_BATCH, DEC_SEQ, D), 1.0),
        'c': nrm((DEC_BATCH, D), 1.0),
        'state_rwkv_fwd': nrm(st_shape, 0.5),
        'state_rwkv_bwd': nrm(st_shape, 0.5),
        'c_ctx': nrm((D,), 1.0),
        'w_mod': nrm((DEPTH, D, 6 * D), 0.5 * D ** -0.5),
        'b_mod': nrm((DEPTH, 6 * D), 0.02),
        'ln_mix_g': 1.0 + nrm((DEPTH, D), 0.02),
        'ln_mix_b': nrm((DEPTH, D), 0.02),
        'ln_ffn_g': 1.0 + nrm((DEPTH, D), 0.02),
        'ln_ffn_b': nrm((DEPTH, D), 0.02),
        'peer_wq': nrm((DEPTH, D, PEER_HEADS * PEER_DK), D ** -0.5),
        'peer_k1': nrm((DEPTH, PEER_HEADS, N_KEYS, PEER_DK // 2), (PEER_DK // 2) ** -0.5),
        'peer_k2': nrm((DEPTH, PEER_HEADS, N_KEYS, PEER_DK // 2), (PEER_DK // 2) ** -0.5),
        'peer_u': nrm((DEPTH, N_EXPERTS, D), D ** -0.5),
        'peer_v': nrm((DEPTH, N_EXPERTS, D), DN_BETA),
        'ab_w_in': nrm((N_EVEN, D, D_AB_IN), D ** -0.5),
        'ab_mu': jax.random.uniform(next(keys), (N_EVEN, D_SHIFTED), F32, 0.2, 0.8),
        'rwkv_w0': -1.0 + nrm((N_EVEN, 2, D_RWKV), 0.5),
        'rwkv_w2': nrm((N_EVEN, 2, DECAY_LORA, D_RWKV), 0.5 * DECAY_LORA ** -0.5),
        'rwkv_a0': nrm((N_EVEN, 2, D_RWKV), 0.5),
        'rwkv_a2': nrm((N_EVEN, 2, AAA_LORA, D_RWKV), 0.5 * AAA_LORA ** -0.5),
        'rwkv_g2': nrm((N_EVEN, GATE_LORA, D_RWKV), GATE_LORA ** -0.5),
        'rwkv_k_k': 1.0 + nrm((N_EVEN, D_RWKV), 0.1),
        'rwkv_k_a': 1.0 + nrm((N_EVEN, D_RWKV), 0.1),
        'rwkv_r_k': nrm((N_EVEN, RWKV_HEADS, HEAD_SIZE), 0.1),
        'rwkv_gn_g': 1.0 + nrm((N_EVEN, D_RWKV), 0.02),
        'rwkv_gn_b': nrm((N_EVEN, D_RWKV), 0.02),
        'ab_w_out': nrm((N_EVEN, D, D), DN_BETA * D ** -0.5),
        'hy_w_in': nrm((N_ODD, D, 3 * D), D ** -0.5),
        'hy_conv_w': nrm((N_ODD, 3, 3 * D), 0.5),
        'hy_conv_b': nrm((N_ODD, 3 * D), 0.02),
        'hy_filt_w1': nrm((N_ODD, HY_EMB, HY_ORDER), HY_EMB ** -0.5),
        'hy_filt_b1': nrm((N_ODD, HY_ORDER), 0.1),
        'hy_filt_w2': nrm((N_ODD, HY_ORDER, HY_ORDER), HY_ORDER ** -0.5),
        'hy_filt_b2': nrm((N_ODD, HY_ORDER), 0.1),
        'hy_filt_w3': nrm((N_ODD, HY_ORDER, 2 * D), 0.1 * HY_ORDER ** -0.5),
        'hy_filt_freq': 1.0 + nrm((N_ODD, HY_ORDER), 0.05),
        'hy_filt_decay': jnp.tile(jnp.linspace(3.07, 15.35, D, dtype=F32), 2)[None, :] * (1.0 + nrm((N_ODD, 2 * D), 0.05)),
        'hy_skip': nrm((N_ODD, D), 0.5),
        'hy_w_out': nrm((N_ODD, D, D), DN_BETA * D ** -0.5),
    }


def reference(x_prompt, x_sample, c, state_rwkv_fwd, state_rwkv_bwd, c_ctx,
              w_mod, b_mod, ln_mix_g, ln_mix_b, ln_ffn_g, ln_ffn_b,
              peer_wq, peer_k1, peer_k2, peer_u, peer_v,
              ab_w_in, ab_mu, rwkv_w0, rwkv_w2, rwkv_a0, rwkv_a2, rwkv_g2,
              rwkv_k_k, rwkv_k_a, rwkv_r_k, rwkv_gn_g, rwkv_gn_b, ab_w_out,
              hy_w_in, hy_conv_w, hy_conv_b, hy_filt_w1, hy_filt_b1, hy_filt_w2,
              hy_filt_b2, hy_filt_w3, hy_filt_freq, hy_filt_decay, hy_skip, hy_w_out):
    P = {
        'w_mod': w_mod, 'b_mod': b_mod,
        'ln_mix_g': ln_mix_g, 'ln_mix_b': ln_mix_b, 'ln_ffn_g': ln_ffn_g, 'ln_ffn_b': ln_ffn_b,
        'peer_wq': peer_wq, 'peer_k1': peer_k1, 'peer_k2': peer_k2, 'peer_u': peer_u, 'peer_v': peer_v,
        'ab_w_in': ab_w_in, 'ab_mu': ab_mu, 'rwkv_w0': rwkv_w0, 'rwkv_w2': rwkv_w2,
        'rwkv_a0': rwkv_a0, 'rwkv_a2': rwkv_a2, 'rwkv_g2': rwkv_g2, 'rwkv_k_k': rwkv_k_k,
        'rwkv_k_a': rwkv_k_a, 'rwkv_r_k': rwkv_r_k, 'rwkv_gn_g': rwkv_gn_g, 'rwkv_gn_b': rwkv_gn_b,
        'ab_w_out': ab_w_out,
        'hy_w_in': hy_w_in, 'hy_conv_w': hy_conv_w, 'hy_conv_b': hy_conv_b,
        'hy_filt_w1': hy_filt_w1, 'hy_filt_b1': hy_filt_b1, 'hy_filt_w2': hy_filt_w2,
        'hy_filt_b2': hy_filt_b2, 'hy_filt_w3': hy_filt_w3, 'hy_filt_freq': hy_filt_freq,
        'hy_filt_decay': hy_filt_decay, 'hy_skip': hy_skip, 'hy_w_out': hy_w_out,
    }
    x_ctx = x_prompt
    c_ctx_vec = c_ctx[None, None, :]
    zero_state = jnp.zeros((x_prompt.shape[0], RWKV_HEADS, HEAD_SIZE, HEAD_SIZE), F32)
    x_lat = x_sample + grid_pos_embed(x_sample.shape[1]).astype(x_sample.dtype)
    c_lat_vec = c[:, None, :]
    new_f, new_b = [], []
    for layer in range(DEPTH):
        x_ctx, s_f, s_b = run_layer(layer, x_ctx, c_ctx_vec, zero_state, zero_state, P)
        if layer % 2 == 0:
            e = layer // 2
            new_f.append(s_f)
            new_b.append(s_b)
            x_lat, _, _ = run_layer(layer, x_lat, c_lat_vec,
                                    state_rwkv_fwd[:, e], state_rwkv_bwd[:, e], P)
        else:
            x_lat, _, _ = run_layer(layer, x_lat, c_lat_vec, None, None, P)
    y_prompt = x_ctx
    y_sample = x_lat
    new_state_rwkv_fwd = jnp.stack(new_f, axis=1)
    new_state_rwkv_bwd = jnp.stack(new_b, axis=1)
    return (y_prompt, y_sample, new_state_rwkv_fwd, new_state_rwkv_bwd)
```

```python
import math
import jax
import jax.numpy as jnp
from jax import lax
from jax.experimental import pallas as pl
from jax.experimental.pallas import tpu as pltpu

D_MODEL = 1024
BATCH = 32
SEQ = 256
DEPTH = 2
DEC_BATCH = 8
DEC_SEQ = 2048
GRID_W = 64
N_EVEN = (DEPTH + 1) // 2
N_ODD = DEPTH // 2
RWKV_HEADS = 12
HEAD_SIZE = 64
D_RWKV = RWKV_HEADS * HEAD_SIZE
DECAY_LORA = 64
AAA_LORA = 64
GATE_LORA = 128
GN_EPS = 64e-5
FNET_GROUPS = 4
FNET_GROUP_W = 64
D_FNET = FNET_GROUPS * FNET_GROUP_W
D_SHIFTED = 3 * D_RWKV + 2 * DECAY_LORA + 2 * AAA_LORA + GATE_LORA
D_AB_IN = D_SHIFTED + D_FNET
HY_BANDS = 16
HY_EMB = 2 * HY_BANDS + 1
HY_ORDER = 64
PEER_HEADS = 8
N_KEYS = 128
N_EXPERTS = N_KEYS * N_KEYS
PEER_DK = 256
PEER_TOPK = 16
PEER_BLOCK = 128
LN_EPS = 1e-5
DN_ALPHA = (2 * DEPTH) ** 0.25
DN_BETA = (8 * DEPTH) ** -0.25
F32 = jnp.float32


def layer_norm(x, g=None, b=None):
    xf = x.astype(F32)
    mu = jnp.mean(xf, axis=-1, keepdims=True)
    var = jnp.mean(jnp.square(xf - mu), axis=-1, keepdims=True)
    y = (xf - mu) * lax.rsqrt(var + LN_EPS)
    if g is not None:
        y = y * g.astype(F32) + b.astype(F32)
    return y.astype(x.dtype)


def shift_prev(z):
    return jnp.pad(z, ((0, 0), (1, 0), (0, 0)))[:, :-1]


def shift_next(z):
    return jnp.pad(z, ((0, 0), (0, 1), (0, 0)))[:, 1:]


def modulation(cvec, w_mod, b_mod):
    m = jax.nn.silu(cvec) @ w_mod + b_mod
    return jnp.split(m, 6, axis=-1)


def grid_pos_embed(n_tokens):
    rows = n_tokens // GRID_W
    row = jnp.repeat(jnp.arange(rows, dtype=F32), GRID_W)
    col = jnp.tile(jnp.arange(GRID_W, dtype=F32), rows)
    quarter = D_MODEL // 4
    omega = 1.0 / (10000.0 ** (jnp.arange(quarter, dtype=F32) / quarter))
    ra = row[:, None] * omega[None, :]
    ca = col[:, None] * omega[None, :]
    return jnp.concatenate([jnp.sin(ra), jnp.cos(ra), jnp.sin(ca), jnp.cos(ca)], axis=-1)


def wkv7_scan(r, decay, k, v, kk, a, s0, reverse):
    def step(s, inp):
        r_t, w_t, k_t, v_t, kk_t, a_t = inp
        sa = jnp.einsum('bhij,bhj->bhi', s, -kk_t)
        s = (s * w_t[:, :, None, :]
             + sa[..., None] * (kk_t * a_t)[:, :, None, :]
             + v_t[..., None] * k_t[:, :, None, :])
        return s, jnp.einsum('bhij,bhj->bhi', s, r_t)
    xs = tuple(jnp.swapaxes(t, 0, 1) for t in (r, decay, k, v, kk, a))
    s_fin, ys = lax.scan(step, s0, xs, reverse=reverse)
    return jnp.swapaxes(ys, 0, 1), s_fin


def rwkv_mix(zs, e, P, s0_f, s0_b):
    B, L, _ = zs.shape
    splits = [D_RWKV, 2 * D_RWKV, 3 * D_RWKV,
              3 * D_RWKV + DECAY_LORA, 3 * D_RWKV + 2 * DECAY_LORA,
              3 * D_RWKV + 2 * DECAY_LORA + AAA_LORA, 3 * D_RWKV + 2 * DECAY_LORA + 2 * AAA_LORA]
    r, k, v, zw_f, zw_b, za_f, za_b, zg = jnp.split(zs, splits, axis=-1)

    def heads(t):
        return t.astype(F32).reshape(B, L, RWKV_HEADS, HEAD_SIZE)

    r_h, k_h, v_h = heads(r), heads(k), heads(v)
    kk = heads(k * P['rwkv_k_k'][e])
    kk = kk * lax.rsqrt(jnp.sum(kk * kk, axis=-1, keepdims=True) + 1e-12)
    k_a = P['rwkv_k_a'][e].astype(F32).reshape(RWKV_HEADS, HEAD_SIZE)
    ys, finals, k_dirs = [], [], []
    for d, (zw, za, s0, rev) in enumerate(((zw_f, za_f, s0_f, False), (zw_b, za_b, s0_b, True))):
        w_raw = (P['rwkv_w0'][e, d] + jnp.tanh(zw) @ P['rwkv_w2'][e, d]).astype(F32)
        decay = heads(jnp.exp(-jnp.exp(-jax.nn.softplus(-w_raw) - 0.5)))
        a = heads(jax.nn.sigmoid((P['rwkv_a0'][e, d] + za @ P['rwkv_a2'][e, d]).astype(F32)))
        k_d = k_h * (1.0 + (a - 1.0) * k_a)
        y_d, s_d = wkv7_scan(r_h, decay, k_d, v_h, kk, a, s0.astype(F32), rev)
        ys.append(y_d)
        finals.append(s_d)
        k_dirs.append(k_d)
    y = ys[0] + ys[1]
    mu = jnp.mean(y, axis=-1, keepdims=True)
    var = jnp.mean(jnp.square(y - mu), axis=-1, keepdims=True)
    y = ((y - mu) * lax.rsqrt(var + GN_EPS)).reshape(B, L, D_RWKV)
    y = y * P['rwkv_gn_g'][e].astype(F32) + P['rwkv_gn_b'][e].astype(F32)
    k_bonus = 0.5 * (k_dirs[0] + k_dirs[1])
    bonus = jnp.sum(r_h * k_bonus * P['rwkv_r_k'][e].astype(F32), axis=-1, keepdims=True) * v_h
    y = y + bonus.reshape(B, L, D_RWKV)
    g = jax.nn.sigmoid(zg) @ P['rwkv_g2'][e]
    return y.astype(zs.dtype) * g, finals[0], finals[1]


def fourier_mix(zf):
    B, L, _ = zf.shape
    z = zf.astype(F32).reshape(B, L, FNET_GROUPS, FNET_GROUP_W)
    y = jnp.fft.fft2(z, axes=(1, 3), norm='ortho').real
    return y.reshape(B, L, D_FNET).astype(zf.dtype)


def ab_mixer(h, e, P, s0_f, s0_b):
    z = h @ P['ab_w_in'][e]
    zs, zf = z[..., :D_SHIFTED], z[..., D_SHIFTED:]
    zs = zs + P['ab_mu'][e] * (0.5 * (shift_prev(zs) + shift_next(zs)) - zs)
    y_a, s_f, s_b = rwkv_mix(zs, e, P, s0_f, s0_b)
    y_b = fourier_mix(zf)
    y = jnp.concatenate([y_a, y_b], axis=-1) @ P['ab_w_out'][e]
    return y, s_f, s_b


def hyena_filters(L, w1, b1, w2, b2, w3, freq, decay):
    pos = jnp.arange(L, dtype=F32)[:, None]
    t = pos / (L - 1)
    bands = jnp.linspace(1e-4, HY_BANDS - 1, HY_BANDS, dtype=F32)[None, :]
    ang = 2.0 * math.pi * pos / L * bands
    feats = jnp.concatenate([t, jnp.cos(ang), -jnp.sin(ang)], axis=-1)
    fr = freq.astype(F32)
    hid = jnp.sin(fr * (feats @ w1.astype(F32) + b1.astype(F32)))
    hid = jnp.sin(fr * (hid @ w2.astype(F32) + b2.astype(F32)))
    filt = (hid @ w3.astype(F32)) * jnp.exp(-t * jnp.abs(decay.astype(F32)))
    return filt[:, :D_MODEL], filt[:, D_MODEL:]


def long_conv(u, h_f, h_b, skip):
    L, D = h_f.shape
    taps = jnp.concatenate([h_f, jnp.zeros((1, D), F32), h_b[:0:-1]], axis=0)
    uf = jnp.fft.rfft(u, n=2 * L, axis=1)
    tf = jnp.fft.rfft(taps, axis=0)
    y = jnp.fft.irfft(uf * tf[None], n=2 * L, axis=1)[:, :L]
    return y + u * skip


def hyena_mixer(h, o, P):
    L = h.shape[1]
    z = h @ P['hy_w_in'][o]
    cw = P['hy_conv_w'][o]
    z = cw[0] * shift_prev(z) + cw[1] * z + cw[2] * shift_next(z) + P['hy_conv_b'][o]
    x0, x1, v = jnp.split(z, 3, axis=-1)
    h_f, h_b = hyena_filters(L, P['hy_filt_w1'][o], P['hy_filt_b1'][o], P['hy_filt_w2'][o],
                             P['hy_filt_b2'][o], P['hy_filt_w3'][o], P['hy_filt_freq'][o],
                             P['hy_filt_decay'][o])
    y = long_conv((v * x1).astype(F32), h_f, h_b, P['hy_skip'][o].astype(F32))
    return (y.astype(h.dtype) * x0) @ P['hy_w_out'][o]


BF16 = jnp.bfloat16
NEG_INF = float('-inf')
VMEM_LIMIT_BYTES = 56 * 1024 * 1024
ROUTE_TT = 256
EXPERT_TT = 512
EXPERT_ET = 512


def _top_rows(x, n):
    row = lax.broadcasted_iota(jnp.int32, (n, x.shape[1]), 0)
    out = jnp.zeros((n, x.shape[1]), F32)
    for j in range(n):
        m = jnp.max(x, axis=0, keepdims=True)
        out = jnp.where(row == j, m, out)
        x = jnp.where(x == m, NEG_INF, x)
    return out


def _route_kernel(xT_ref, wqT_ref, k1_ref, k2_ref, s1_ref, s2_ref, p1_ref, p2_ref, tau_ref, qT_ref):
    qT_ref[...] = jnp.dot(wqT_ref[...], xT_ref[...], preferred_element_type=F32)
    half = PEER_DK // 2
    lanes = 128

    def head(h, carry):
        base = pl.multiple_of(h * PEER_DK, PEER_DK)
        q1 = qT_ref[pl.ds(base, half), :].astype(BF16)
        q2 = qT_ref[pl.ds(base + half, half), :].astype(BF16)
        s1 = jnp.dot(k1_ref[h], q1, preferred_element_type=F32)
        s2 = jnp.dot(k2_ref[h], q2, preferred_element_type=F32)
        s1_ref[h] = s1
        s2_ref[h] = s2
        for c in range(s1.shape[1] // lanes):
            sl = slice(c * lanes, (c + 1) * lanes)
            s1c, s2c = s1[:, sl], s2[:, sl]
            a = _top_rows(s1c, PEER_TOPK)
            b = _top_rows(s2c, PEER_TOPK)
            parts = [a[0:1] + b]
            for i in range(1, 8):
                parts.append(a[i:i + 1] + b[0:8])
            parts.append(a[8:16] + b[0:1])
            cand = jnp.concatenate(parts, axis=0)
            top = cand[0:1]
            work = cand
            tau = jnp.zeros_like(top)
            left = jnp.full(top.shape, float(PEER_TOPK), F32)
            for _ in range(PEER_TOPK):
                m = jnp.max(work, axis=0, keepdims=True)
                hit = work == m
                tau = jnp.where(left > 0.0, m, tau)
                left = left - jnp.sum(jnp.where(hit, 1.0, 0.0), axis=0, keepdims=True)
                work = jnp.where(hit, NEG_INF, work)
            z = jnp.sum(jnp.where(cand >= tau, jnp.exp(cand - top), 0.0), axis=0, keepdims=True)
            tau_ref[h, :, sl] = tau
            p1_ref[h, :, sl] = jnp.exp(s1c - a[0:1]) / z
            p2_ref[h, :, sl] = jnp.exp(s2c - b[0:1])
        return carry

    lax.fori_loop(0, PEER_HEADS, head, 0)


def _expert_kernel(xT_ref, s1_ref, s2_ref, p1_ref, p2_ref, tau_ref, u_ref, vT_ref, o_ref, w_ref):
    i = pl.program_id(1)

    @pl.when(i == 0)
    def _():
        o_ref[...] = jnp.zeros_like(o_ref)

    n1 = u_ref.shape[0] // N_KEYS
    pre = jnp.dot(u_ref[...], xT_ref[...], preferred_element_type=F32)
    for g in range(n1):
        i1 = i * n1 + g
        gate = jnp.zeros((N_KEYS, xT_ref.shape[1]), F32)
        for h in range(PEER_HEADS):
            s = s1_ref[h, pl.ds(i1, 1), :] + s2_ref[h]
            p = p1_ref[h, pl.ds(i1, 1), :] * p2_ref[h]
            gate = gate + jnp.where(s >= tau_ref[h], p, 0.0)
        act = jax.nn.gelu(pre[g * N_KEYS:(g + 1) * N_KEYS])
        w_ref[g * N_KEYS:(g + 1) * N_KEYS, :] = (act * gate).astype(BF16)
    o_ref[...] += jnp.dot(vT_ref[...], w_ref[...], preferred_element_type=F32)


def peer_pallas(hT, wqT, k1, k2, u, vT):
    D, T = hT.shape
    H, NK = PEER_HEADS, N_KEYS
    route_shapes = [jax.ShapeDtypeStruct((H, NK, T), F32)] * 4 + [jax.ShapeDtypeStruct((H, 1, T), F32)]
    tt = ROUTE_TT
    s1, s2, p1, p2, tau = pl.pallas_call(
        _route_kernel,
        grid=(T // tt,),
        in_specs=[pl.BlockSpec((D, tt), lambda j: (0, j)),
                  pl.BlockSpec((H * PEER_DK, D), lambda j: (0, 0)),
                  pl.BlockSpec((H, NK, PEER_DK // 2), lambda j: (0, 0, 0)),
                  pl.BlockSpec((H, NK, PEER_DK // 2), lambda j: (0, 0, 0))],
        out_specs=[pl.BlockSpec((H, NK, tt), lambda j: (0, 0, j))] * 4 + [pl.BlockSpec((H, 1, tt), lambda j: (0, 0, j))],
        out_shape=route_shapes,
        scratch_shapes=[pltpu.VMEM((H * PEER_DK, tt), F32)],
        compiler_params=pltpu.CompilerParams(dimension_semantics=("arbitrary",),
                                             vmem_limit_bytes=VMEM_LIMIT_BYTES),
        name="peer_route",
    )(hT, wqT, k1, k2)
    tt, et = EXPERT_TT, EXPERT_ET
    route_spec = pl.BlockSpec((H, NK, tt), lambda j, i: (0, 0, j))
    return pl.pallas_call(
        _expert_kernel,
        grid=(T // tt, N_EXPERTS // et),
        in_specs=[pl.BlockSpec((D, tt), lambda j, i: (0, j)),
                  route_spec, route_spec, route_spec, route_spec,
                  pl.BlockSpec((H, 1, tt), lambda j, i: (0, 0, j)),
                  pl.BlockSpec((et, D), lambda j, i: (i, 0)),
                  pl.BlockSpec((D, et), lambda j, i: (0, i))],
        out_specs=pl.BlockSpec((D, tt), lambda j, i: (0, j)),
        out_shape=jax.ShapeDtypeStruct((D, T), F32),
        scratch_shapes=[pltpu.VMEM((et, tt), BF16)],
        compiler_params=pltpu.CompilerParams(dimension_semantics=("arbitrary", "arbitrary"),
                                             vmem_limit_bytes=VMEM_LIMIT_BYTES),
        name="peer_experts",
    )(hT, s1, s2, p1, p2, tau, u, vT)


def peer(h, layer, P):
    B, L, D = h.shape
    hT = h.reshape(B * L, D).T.astype(BF16)
    outT = peer_pallas(hT, P['peer_wq'][layer].T.astype(BF16),
                       P['peer_k1'][layer].astype(BF16), P['peer_k2'][layer].astype(BF16),
                       P['peer_u'][layer].astype(BF16), P['peer_v'][layer].T.astype(BF16))
    return outT.T.reshape(B, L, D)


def run_layer(layer, x, cvec, s0_f, s0_b, P):
    sh1, sc1, g1, sh2, sc2, g2 = modulation(cvec, P['w_mod'][layer], P['b_mod'][layer])
    h = layer_norm(x) * (1.0 + sc1) + sh1
    if layer % 2 == 0:
        y, s_f, s_b = ab_mixer(h, layer // 2, P, s0_f, s0_b)
    else:
        y = hyena_mixer(h, layer // 2, P)
        s_f, s_b = None, None
    x = layer_norm(DN_ALPHA * x + g1 * y, P['ln_mix_g'][layer], P['ln_mix_b'][layer])
    h = layer_norm(x) * (1.0 + sc2) + sh2
    y = peer(h, layer, P)
    x = layer_norm(DN_ALPHA * x + g2 * y, P['ln_ffn_g'][layer], P['ln_ffn_b'][layer])
    return x, s_f, s_b


def _copy_kernel(x_ref, o_ref):
    o_ref[...] = x_ref[...]


def _pallas_copy(x):
    T, D = x.shape
    return pl.pallas_call(
        _copy_kernel, grid=(T // 512,),
        in_specs=[pl.BlockSpec((512, D), lambda i: (i, 0))],
        out_specs=pl.BlockSpec((512, D), lambda i: (i, 0)),
        out_shape=jax.ShapeDtypeStruct(x.shape, x.dtype))(x)


def kernel(x_prompt, x_sample, c, state_rwkv_fwd, state_rwkv_bwd, c_ctx, w_mod, b_mod, ln_mix_g, ln_mix_b, ln_ffn_g, ln_ffn_b, peer_wq, peer_k1, peer_k2, peer_u, peer_v, ab_w_in, ab_mu, rwkv_w0, rwkv_w2, rwkv_a0, rwkv_a2, rwkv_g2, rwkv_k_k, rwkv_k_a, rwkv_r_k, rwkv_gn_g, rwkv_gn_b, ab_w_out, hy_w_in, hy_conv_w, hy_conv_b, hy_filt_w1, hy_filt_b1, hy_filt_w2, hy_filt_b2, hy_filt_w3, hy_filt_freq, hy_filt_decay, hy_skip, hy_w_out):
    P = {
        'w_mod': w_mod, 'b_mod': b_mod,
        'ln_mix_g': ln_mix_g, 'ln_mix_b': ln_mix_b, 'ln_ffn_g': ln_ffn_g, 'ln_ffn_b': ln_ffn_b,
        'peer_wq': peer_wq, 'peer_k1': peer_k1, 'peer_k2': peer_k2, 'peer_u': peer_u, 'peer_v': peer_v,
        'ab_w_in': ab_w_in, 'ab_mu': ab_mu, 'rwkv_w0': rwkv_w0, 'rwkv_w2': rwkv_w2,
        'rwkv_a0': rwkv_a0, 'rwkv_a2': rwkv_a2, 'rwkv_g2': rwkv_g2, 'rwkv_k_k': rwkv_k_k,
        'rwkv_k_a': rwkv_k_a, 'rwkv_r_k': rwkv_r_k, 'rwkv_gn_g': rwkv_gn_g, 'rwkv_gn_b': rwkv_gn_b,
        'ab_w_out': ab_w_out,
        'hy_w_in': hy_w_in, 'hy_conv_w': hy_conv_w, 'hy_conv_b': hy_conv_b,
        'hy_filt_w1': hy_filt_w1, 'hy_filt_b1': hy_filt_b1, 'hy_filt_w2': hy_filt_w2,
        'hy_filt_b2': hy_filt_b2, 'hy_filt_w3': hy_filt_w3, 'hy_filt_freq': hy_filt_freq,
        'hy_filt_decay': hy_filt_decay, 'hy_skip': hy_skip, 'hy_w_out': hy_w_out,
    }
    x_ctx = x_prompt
    c_ctx_vec = c_ctx[None, None, :]
    zero_state = jnp.zeros((x_prompt.shape[0], RWKV_HEADS, HEAD_SIZE, HEAD_SIZE), F32)
    x_lat = x_sample + grid_pos_embed(x_sample.shape[1]).astype(x_sample.dtype)
    c_lat_vec = c[:, None, :]
    new_f, new_b = [], []
    for layer in range(DEPTH):
        x_ctx, s_f, s_b = run_layer(layer, x_ctx, c_ctx_vec, zero_state, zero_state, P)
        if layer % 2 == 0:
            e = layer // 2
            new_f.append(s_f)
            new_b.append(s_b)
            x_lat, _, _ = run_layer(layer, x_lat, c_lat_vec,
                                    state_rwkv_fwd[:, e], state_rwkv_bwd[:, e], P)
        else:
            x_lat, _, _ = run_layer(layer, x_lat, c_lat_vec, None, None, P)
    y_prompt = _pallas_copy(x_ctx.reshape(-1, D_MODEL)).reshape(x_ctx.shape)
    return (y_prompt, x_lat, jnp.stack(new_f, axis=1), jnp.stack(new_b, axis=1))
```

```python
import functools
import math
import jax
import jax.numpy as jnp
from jax import lax
from jax.experimental import pallas as pl
from jax.experimental.pallas import tpu as pltpu

D_MODEL = 1024
BATCH = 32
SEQ = 256
DEPTH = 2
DEC_BATCH = 8
DEC_SEQ = 2048
GRID_W = 64
N_EVEN = (DEPTH + 1) // 2
N_ODD = DEPTH // 2
RWKV_HEADS = 12
HEAD_SIZE = 64
D_RWKV = RWKV_HEADS * HEAD_SIZE
DECAY_LORA = 64
AAA_LORA = 64
GATE_LORA = 128
GN_EPS = 64e-5
FNET_GROUPS = 4
FNET_GROUP_W = 64
D_FNET = FNET_GROUPS * FNET_GROUP_W
D_SHIFTED = 3 * D_RWKV + 2 * DECAY_LORA + 2 * AAA_LORA + GATE_LORA
D_AB_IN = D_SHIFTED + D_FNET
HY_BANDS = 16
HY_EMB = 2 * HY_BANDS + 1
HY_ORDER = 64
PEER_HEADS = 8
N_KEYS = 128
N_EXPERTS = N_KEYS * N_KEYS
PEER_DK = 256
PEER_TOPK = 16
PEER_BLOCK = 128
LN_EPS = 1e-5
DN_ALPHA = (2 * DEPTH) ** 0.25
DN_BETA = (8 * DEPTH) ** -0.25
F32 = jnp.float32


def layer_norm(x, g=None, b=None):
    xf = x.astype(F32)
    mu = jnp.mean(xf, axis=-1, keepdims=True)
    var = jnp.mean(jnp.square(xf - mu), axis=-1, keepdims=True)
    y = (xf - mu) * lax.rsqrt(var + LN_EPS)
    if g is not None:
        y = y * g.astype(F32) + b.astype(F32)
    return y.astype(x.dtype)


def shift_prev(z):
    return jnp.pad(z, ((0, 0), (1, 0), (0, 0)))[:, :-1]


def shift_next(z):
    return jnp.pad(z, ((0, 0), (0, 1), (0, 0)))[:, 1:]


def modulation(cvec, w_mod, b_mod):
    m = jax.nn.silu(cvec) @ w_mod + b_mod
    return jnp.split(m, 6, axis=-1)


def grid_pos_embed(n_tokens):
    rows = n_tokens // GRID_W
    row = jnp.repeat(jnp.arange(rows, dtype=F32), GRID_W)
    col = jnp.tile(jnp.arange(GRID_W, dtype=F32), rows)
    quarter = D_MODEL // 4
    omega = 1.0 / (10000.0 ** (jnp.arange(quarter, dtype=F32) / quarter))
    ra = row[:, None] * omega[None, :]
    ca = col[:, None] * omega[None, :]
    return jnp.concatenate([jnp.sin(ra), jnp.cos(ra), jnp.sin(ca), jnp.cos(ca)], axis=-1)


BF16 = jnp.bfloat16
VMEM_LIMIT_BYTES = 56 * 1024 * 1024
LANES = 128
WKV_CHUNK = 16
WKV_UNROLL = 4


def _split(x):
    hi = x.astype(BF16)
    return hi, (x - hi.astype(F32)).astype(BF16)


def _dot3(a, b, dims=(((1,), (0,)), ((), ()))):
    ah, al = _split(a)
    bh, bl = _split(b)
    mm = functools.partial(lax.dot_general, dimension_numbers=dims, preferred_element_type=F32)
    return mm(ah, bh) + (mm(al, bh) + mm(ah, bl))


_NT = (((1,), (1,)), ((), ()))


def _pair_rows(z, lo):
    return jnp.concatenate([jnp.where(lo, z, 0.0), jnp.where(lo, 0.0, z)], axis=0)


def _chunk_transition(lw, ak, kka, k, r, v, rev):
    C = WKV_CHUNK
    t_i = lax.broadcasted_iota(jnp.int32, (C, C), 0)
    s_i = lax.broadcasted_iota(jnp.int32, (C, C), 1)
    upto = (s_i >= t_i) if rev else (s_i <= t_i)
    cum = _dot3(jnp.where(upto, 1.0, 0.0), lw)
    yield
    tot = cum[0:1] if rev else cum[C - 1:C]
    a_t = ak * jnp.exp(cum - lw)
    r_t = r * jnp.exp(cum)
    inv = jnp.exp(-cum)
    b_t, k_t = kka * inv, k * inv
    end = jnp.exp(tot - cum)
    b_e, k_e = kka * end, k * end
    lo = lax.broadcasted_iota(jnp.int32, (1, LANES), 1) < HEAD_SIZE
    lo2 = jnp.concatenate([lo, lo], axis=1)
    t2 = lax.broadcasted_iota(jnp.int32, (C, 2 * C), 0)
    s2 = lax.broadcasted_iota(jnp.int32, (C, 2 * C), 1) % C
    before2 = (s2 > t2) if rev else (s2 < t2)
    upto2 = (s2 >= t2) if rev else (s2 <= t2)
    x = jnp.concatenate([a_t, r_t], axis=0)
    gb = _dot3(x, _pair_rows(b_t, lo), _NT)
    gk = _dot3(x, _pair_rows(k_t, lo), _NT)
    nt = _dot3(b_t, _pair_rows(a_t, lo), _NT)
    yield
    lak = jnp.where(before2, gk[0:C], 0.0)
    lrb = jnp.where(upto2, gb[C:2 * C], 0.0)
    lrk = jnp.where(upto2, gk[C:2 * C], 0.0)
    nt = jnp.where((t2 > s2) if rev else (t2 < s2), nt, 0.0)
    lane2 = lax.broadcasted_iota(jnp.int32, (C, 2 * C), 1)
    row2 = t2
    lakv = _dot3(lak, _pair_rows(v, lo))
    y1 = _dot3(lrk, _pair_rows(v, lo))
    tinv = jnp.zeros((C, 2 * C), F32)
    for t in (range(C - 1, -1, -1) if rev else range(C)):
        coef = jnp.take_along_axis(nt, t + C * (lane2 // C), axis=1)
        new = jnp.where(lane2 % C == t, 1.0, 0.0)[0:1] + jnp.sum(coef * tinv, axis=0, keepdims=True)
        tinv = jnp.where(row2 == t, new, tinv)
        yield
    m1w2 = _dot3(tinv, _pair_rows(jnp.concatenate([a_t, lakv], axis=1), lo2))
    yield
    my = _dot3(lrb, _pair_rows(m1w2, lo2))
    z = jnp.concatenate([b_e, k_e], axis=0)
    rhs = jnp.concatenate([m1w2, jnp.concatenate([jnp.zeros_like(v), v], axis=1)], axis=0)
    p = _dot3(z.T, rhs)
    yield
    m2 = r_t + my[:, :LANES]
    y1 = my[:, LANES:] + y1
    r_l = lax.broadcasted_iota(jnp.int32, (LANES, LANES), 0)
    c_l = lax.broadcasted_iota(jnp.int32, (LANES, LANES), 1)
    same = (r_l < HEAD_SIZE) == (c_l < HEAD_SIZE)
    m3 = jnp.where(same, p[:, :LANES], 0.0) + jnp.where(r_l == c_l, jnp.exp(tot), 0.0)
    s1 = jnp.where(same, p[:, LANES:], 0.0)
    yield m2, y1, m3, s1


def _lockstep(gens):
    out = [None] * len(gens)
    live = list(range(len(gens)))
    while live:
        for i in list(live):
            try:
                val = next(gens[i])
            except StopIteration:
                live.remove(i)
            else:
                out[i] = val if val is not None else out[i]
    return out


def _wkv_kernel(ak_ref, r_ref, v_ref, lwf_ref, kkaf_ref, kf_ref, lwb_ref, kkab_ref, kb_ref, s0f_ref, s0b_ref,
                y_ref, sf_ref, sb_ref, yb_ref):
    L = v_ref.shape[1]
    C = WKV_CHUNK
    n = L // C

    def load(ref, o):
        return ref[0, pl.ds(o, C), :]

    def body(c, carry):
        sf, sb = carry
        offs_f = [pl.multiple_of((c * WKV_UNROLL + i) * C, C) for i in range(WKV_UNROLL)]
        offs_b = [pl.multiple_of((n - 1 - c * WKV_UNROLL - i) * C, C) for i in range(WKV_UNROLL)]
        gf = [_chunk_transition(load(lwf_ref, o), load(ak_ref, o), load(kkaf_ref, o), load(kf_ref, o),
                                load(r_ref, o), load(v_ref, o), False) for o in offs_f]
        gb = [_chunk_transition(load(lwb_ref, o), load(ak_ref, o), load(kkab_ref, o), load(kb_ref, o),
                                load(r_ref, o), load(v_ref, o), True) for o in offs_b]
        done = _lockstep(gf + gb)
        tf, tb = done[:WKV_UNROLL], done[WKV_UNROLL:]
        for i in range(WKV_UNROLL):
            m2, y1, m3, s1 = tf[i]
            y_ref[0, pl.ds(offs_f[i], C), :] = _dot3(m2, sf) + y1
            sf = _dot3(m3, sf) + s1
            m2, y1, m3, s1 = tb[i]
            yb_ref[pl.ds(offs_b[i], C), :] = _dot3(m2, sb) + y1
            sb = _dot3(m3, sb) + s1
        return sf, sb

    sf, sb = lax.fori_loop(0, n // WKV_UNROLL, body, (s0f_ref[0, 0], s0b_ref[0, 0]))
    sf_ref[0, 0] = sf
    sb_ref[0, 0] = sb
    y_ref[0] = y_ref[0] + yb_ref[...]


def wkv_pallas(ak, r, v, lwf, kkaf, kf, lwb, kkab, kb, s0f, s0b):
    B, L, Cd = v.shape
    npair = Cd // LANES
    seq = pl.BlockSpec((1, L, LANES), lambda b, p: (b, 0, p))
    st = pl.BlockSpec((1, 1, LANES, LANES), lambda b, p: (b, p, 0, 0))
    st_shape = jax.ShapeDtypeStruct((B, npair, LANES, LANES), F32)
    return pl.pallas_call(
        _wkv_kernel, grid=(B, npair),
        in_specs=[seq] * 9 + [st, st],
        out_specs=[seq, st, st],
        out_shape=[jax.ShapeDtypeStruct((B, L, Cd), F32), st_shape, st_shape],
        scratch_shapes=[pltpu.VMEM((L, LANES), F32)],
        compiler_params=pltpu.CompilerParams(dimension_semantics=("arbitrary", "arbitrary"),
                                             vmem_limit_bytes=VMEM_LIMIT_BYTES),
        name="wkv_scan",
    )(ak, r, v, lwf, kkaf, kf, lwb, kkab, kb, s0f, s0b)


def _pack_state(s):
    B = s.shape[0]
    st = jnp.swapaxes(s, -1, -2).reshape(B, RWKV_HEADS // 2, 2, HEAD_SIZE, HEAD_SIZE)
    z = jnp.zeros_like(st[:, :, 0])
    top = jnp.concatenate([st[:, :, 0], z], axis=-1)
    bot = jnp.concatenate([z, st[:, :, 1]], axis=-1)
    return jnp.concatenate([top, bot], axis=-2)


def _unpack_state(sd):
    B = sd.shape[0]
    a = sd[:, :, :HEAD_SIZE, :HEAD_SIZE]
    b = sd[:, :, HEAD_SIZE:, HEAD_SIZE:]
    st = jnp.stack([a, b], axis=2).reshape(B, RWKV_HEADS, HEAD_SIZE, HEAD_SIZE)
    return jnp.swapaxes(st, -1, -2)


def rwkv_mix(zs, e, P, s0_f, s0_b):
    B, L, _ = zs.shape
    splits = [D_RWKV, 2 * D_RWKV, 3 * D_RWKV,
              3 * D_RWKV + DECAY_LORA, 3 * D_RWKV + 2 * DECAY_LORA,
              3 * D_RWKV + 2 * DECAY_LORA + AAA_LORA, 3 * D_RWKV + 2 * DECAY_LORA + 2 * AAA_LORA]
    r, k, v, zw_f, zw_b, za_f, za_b, zg = jnp.split(zs, splits, axis=-1)

    def heads(t):
        return t.astype(F32).reshape(B, L, RWKV_HEADS, HEAD_SIZE)

    r_h, k_h, v_h = heads(r), heads(k), heads(v)
    kk = heads(k * P['rwkv_k_k'][e])
    kk = kk * lax.rsqrt(jnp.sum(kk * kk, axis=-1, keepdims=True) + 1e-12)
    k_a = P['rwkv_k_a'][e].astype(F32).reshape(RWKV_HEADS, HEAD_SIZE)
    k_dirs, ops = [], []
    flat = lambda t: t.reshape(B, L, D_RWKV)
    for d, (zw, za) in enumerate(((zw_f, za_f), (zw_b, za_b))):
        w_raw = (P['rwkv_w0'][e, d] + jnp.tanh(zw) @ P['rwkv_w2'][e, d]).astype(F32)
        log_decay = -jnp.exp(-jax.nn.softplus(-w_raw) - 0.5)
        a = heads(jax.nn.sigmoid((P['rwkv_a0'][e, d] + za @ P['rwkv_a2'][e, d]).astype(F32)))
        k_d = k_h * (1.0 + (a - 1.0) * k_a)
        ops += [log_decay, flat(kk * a), flat(k_d)]
        k_dirs.append(k_d)
    y, sd_f, sd_b = wkv_pallas(flat(-kk), flat(r_h), flat(v_h), *ops,
                               _pack_state(s0_f.astype(F32)), _pack_state(s0_b.astype(F32)))
    finals = [_unpack_state(sd_f), _unpack_state(sd_b)]
    y = y.reshape(B, L, RWKV_HEADS, HEAD_SIZE)
    mu = jnp.mean(y, axis=-1, keepdims=True)
    var = jnp.mean(jnp.square(y - mu), axis=-1, keepdims=True)
    y = ((y - mu) * lax.rsqrt(var + GN_EPS)).reshape(B, L, D_RWKV)
    y = y * P['rwkv_gn_g'][e].astype(F32) + P['rwkv_gn_b'][e].astype(F32)
    k_bonus = 0.5 * (k_dirs[0] + k_dirs[1])
    bonus = jnp.sum(r_h * k_bonus * P['rwkv_r_k'][e].astype(F32), axis=-1, keepdims=True) * v_h
    y = y + bonus.reshape(B, L, D_RWKV)
    g = jax.nn.sigmoid(zg) @ P['rwkv_g2'][e]
    return y.astype(zs.dtype) * g, finals[0], finals[1]


def fourier_mix(zf):
    B, L, _ = zf.shape
    z = zf.astype(F32).reshape(B, L, FNET_GROUPS, FNET_GROUP_W)
    y = jnp.fft.fft2(z, axes=(1, 3), norm='ortho').real
    return y.reshape(B, L, D_FNET).astype(zf.dtype)


def ab_mixer(h, e, P, s0_f, s0_b):
    z = h @ P['ab_w_in'][e]
    zs, zf = z[..., :D_SHIFTED], z[..., D_SHIFTED:]
    zs = zs + P['ab_mu'][e] * (0.5 * (shift_prev(zs) + shift_next(zs)) - zs)
    y_a, s_f, s_b = rwkv_mix(zs, e, P, s0_f, s0_b)
    y_b = fourier_mix(zf)
    y = jnp.concatenate([y_a, y_b], axis=-1) @ P['ab_w_out'][e]
    return y, s_f, s_b


def hyena_filters(L, w1, b1, w2, b2, w3, freq, decay):
    pos = jnp.arange(L, dtype=F32)[:, None]
    t = pos / (L - 1)
    bands = jnp.linspace(1e-4, HY_BANDS - 1, HY_BANDS, dtype=F32)[None, :]
    ang = 2.0 * math.pi * pos / L * bands
    feats = jnp.concatenate([t, jnp.cos(ang), -jnp.sin(ang)], axis=-1)
    fr = freq.astype(F32)
    hid = jnp.sin(fr * (feats @ w1.astype(F32) + b1.astype(F32)))
    hid = jnp.sin(fr * (hid @ w2.astype(F32) + b2.astype(F32)))
    filt = (hid @ w3.astype(F32)) * jnp.exp(-t * jnp.abs(decay.astype(F32)))
    return filt[:, :D_MODEL], filt[:, D_MODEL:]


def long_conv(u, h_f, h_b, skip):
    L, D = h_f.shape
    taps = jnp.concatenate([h_f, jnp.zeros((1, D), F32), h_b[:0:-1]], axis=0)
    uf = jnp.fft.rfft(u, n=2 * L, axis=1)
    tf = jnp.fft.rfft(taps, axis=0)
    y = jnp.fft.irfft(uf * tf[None], n=2 * L, axis=1)[:, :L]
    return y + u * skip


def hyena_mixer(h, o, P):
    L = h.shape[1]
    z = h @ P['hy_w_in'][o]
    cw = P['hy_conv_w'][o]
    z = cw[0] * shift_prev(z) + cw[1] * z + cw[2] * shift_next(z) + P['hy_conv_b'][o]
    x0, x1, v = jnp.split(z, 3, axis=-1)
    h_f, h_b = hyena_filters(L, P['hy_filt_w1'][o], P['hy_filt_b1'][o], P['hy_filt_w2'][o],
                             P['hy_filt_b2'][o], P['hy_filt_w3'][o], P['hy_filt_freq'][o],
                             P['hy_filt_decay'][o])
    y = long_conv((v * x1).astype(F32), h_f, h_b, P['hy_skip'][o].astype(F32))
    return (y.astype(h.dtype) * x0) @ P['hy_w_out'][o]


NEG_INF = float('-inf')
ROUTE_TT = 256
EXPERT_TT = 512
EXPERT_ET = 512


def _top_rows(x, n):
    row = lax.broadcasted_iota(jnp.int32, (n, x.shape[1]), 0)
    out = jnp.zeros((n, x.shape[1]), F32)
    for j in range(n):
        m = jnp.max(x, axis=0, keepdims=True)
        out = jnp.where(row == j, m, out)
        x = jnp.where(x == m, NEG_INF, x)
    return out


def _route_kernel(xT_ref, wqT_ref, k1_ref, k2_ref, s1_ref, s2_ref, p1_ref, p2_ref, tau_ref, qT_ref):
    qT_ref[...] = jnp.dot(wqT_ref[...], xT_ref[...], preferred_element_type=F32)
    half = PEER_DK // 2
    lanes = 128

    def head(h, carry):
        base = pl.multiple_of(h * PEER_DK, PEER_DK)
        q1 = qT_ref[pl.ds(base, half), :].astype(BF16)
        q2 = qT_ref[pl.ds(base + half, half), :].astype(BF16)
        s1 = jnp.dot(k1_ref[h], q1, preferred_element_type=F32)
        s2 = jnp.dot(k2_ref[h], q2, preferred_element_type=F32)
        s1_ref[h] = s1
        s2_ref[h] = s2
        for c in range(s1.shape[1] // lanes):
            sl = slice(c * lanes, (c + 1) * lanes)
            s1c, s2c = s1[:, sl], s2[:, sl]
            a = _top_rows(s1c, PEER_TOPK)
            b = _top_rows(s2c, PEER_TOPK)
            parts = [a[0:1] + b]
            for i in range(1, 8):
                parts.append(a[i:i + 1] + b[0:8])
            parts.append(a[8:16] + b[0:1])
            cand = jnp.concatenate(parts, axis=0)
            top = cand[0:1]
            work = cand
            tau = jnp.zeros_like(top)
            left = jnp.full(top.shape, float(PEER_TOPK), F32)
            for _ in range(PEER_TOPK):
                m = jnp.max(work, axis=0, keepdims=True)
                hit = work == m
                tau = jnp.where(left > 0.0, m, tau)
                left = left - jnp.sum(jnp.where(hit, 1.0, 0.0), axis=0, keepdims=True)
                work = jnp.where(hit, NEG_INF, work)
            z = jnp.sum(jnp.where(cand >= tau, jnp.exp(cand - top), 0.0), axis=0, keepdims=True)
            tau_ref[h, :, sl] = tau
            p1_ref[h, :, sl] = jnp.exp(s1c - a[0:1]) / z
            p2_ref[h, :, sl] = jnp.exp(s2c - b[0:1])
        return carry

    lax.fori_loop(0, PEER_HEADS, head, 0)


def _expert_kernel(xT_ref, s1_ref, s2_ref, p1_ref, p2_ref, tau_ref, u_ref, vT_ref, o_ref, w_ref):
    i = pl.program_id(1)

    @pl.when(i == 0)
    def _():
        o_ref[...] = jnp.zeros_like(o_ref)

    n1 = u_ref.shape[0] // N_KEYS
    pre = jnp.dot(u_ref[...], xT_ref[...], preferred_element_type=F32)
    for g in range(n1):
        i1 = i * n1 + g
        gate = jnp.zeros((N_KEYS, xT_ref.shape[1]), F32)
        for h in range(PEER_HEADS):
            s = s1_ref[h, pl.ds(i1, 1), :] + s2_ref[h]
            p = p1_ref[h, pl.ds(i1, 1), :] * p2_ref[h]
            gate = gate + jnp.where(s >= tau_ref[h], p, 0.0)
        act = jax.nn.gelu(pre[g * N_KEYS:(g + 1) * N_KEYS])
        w_ref[g * N_KEYS:(g + 1) * N_KEYS, :] = (act * gate).astype(BF16)
    o_ref[...] += jnp.dot(vT_ref[...], w_ref[...], preferred_element_type=F32)


def peer_pallas(hT, wqT, k1, k2, u, vT):
    D, T = hT.shape
    H, NK = PEER_HEADS, N_KEYS
    route_shapes = [jax.ShapeDtypeStruct((H, NK, T), F32)] * 4 + [jax.ShapeDtypeStruct((H, 1, T), F32)]
    tt = ROUTE_TT
    s1, s2, p1, p2, tau = pl.pallas_call(
        _route_kernel,
        grid=(T // tt,),
        in_specs=[pl.BlockSpec((D, tt), lambda j: (0, j)),
                  pl.BlockSpec((H * PEER_DK, D), lambda j: (0, 0)),
                  pl.BlockSpec((H, NK, PEER_DK // 2), lambda j: (0, 0, 0)),
                  pl.BlockSpec((H, NK, PEER_DK // 2), lambda j: (0, 0, 0))],
        out_specs=[pl.BlockSpec((H, NK, tt), lambda j: (0, 0, j))] * 4 + [pl.BlockSpec((H, 1, tt), lambda j: (0, 0, j))],
        out_shape=route_shapes,
        scratch_shapes=[pltpu.VMEM((H * PEER_DK, tt), F32)],
        compiler_params=pltpu.CompilerParams(dimension_semantics=("arbitrary",),
                                             vmem_limit_bytes=VMEM_LIMIT_BYTES),
        name="peer_route",
    )(hT, wqT, k1, k2)
    tt, et = EXPERT_TT, EXPERT_ET
    route_spec = pl.BlockSpec((H, NK, tt), lambda j, i: (0, 0, j))
    return pl.pallas_call(
        _expert_kernel,
        grid=(T // tt, N_EXPERTS // et),
        in_specs=[pl.BlockSpec((D, tt), lambda j, i: (0, j)),
                  route_spec, route_spec, route_spec, route_spec,
                  pl.BlockSpec((H, 1, tt), lambda j, i: (0, 0, j)),
                  pl.BlockSpec((et, D), lambda j, i: (i, 0)),
                  pl.BlockSpec((D, et), lambda j, i: (0, i))],
        out_specs=pl.BlockSpec((D, tt), lambda j, i: (0, j)),
        out_shape=jax.ShapeDtypeStruct((D, T), F32),
        scratch_shapes=[pltpu.VMEM((et, tt), BF16)],
        compiler_params=pltpu.CompilerParams(dimension_semantics=("arbitrary", "arbitrary"),
                                             vmem_limit_bytes=VMEM_LIMIT_BYTES),
        name="peer_experts",
    )(hT, s1, s2, p1, p2, tau, u, vT)


def peer(h, layer, P):
    B, L, D = h.shape
    hT = h.reshape(B * L, D).T.astype(BF16)
    outT = peer_pallas(hT, P['peer_wq'][layer].T.astype(BF16),
                       P['peer_k1'][layer].astype(BF16), P['peer_k2'][layer].astype(BF16),
                       P['peer_u'][layer].astype(BF16), P['peer_v'][layer].T.astype(BF16))
    return outT.T.reshape(B, L, D)


def run_layer(layer, x, cvec, s0_f, s0_b, P):
    sh1, sc1, g1, sh2, sc2, g2 = modulation(cvec, P['w_mod'][layer], P['b_mod'][layer])
    h = layer_norm(x) * (1.0 + sc1) + sh1
    if layer % 2 == 0:
        y, s_f, s_b = ab_mixer(h, layer // 2, P, s0_f, s0_b)
    else:
        y = hyena_mixer(h, layer // 2, P)
        s_f, s_b = None, None
    x = layer_norm(DN_ALPHA * x + g1 * y, P['ln_mix_g'][layer], P['ln_mix_b'][layer])
    h = layer_norm(x) * (1.0 + sc2) + sh2
    y = peer(h, layer, P)
    x = layer_norm(DN_ALPHA * x + g2 * y, P['ln_ffn_g'][layer], P['ln_ffn_b'][layer])
    return x, s_f, s_b


def _copy_kernel(x_ref, o_ref):
    o_ref[...] = x_ref[...]


def _pallas_copy(x):
    T, D = x.shape
    return pl.pallas_call(
        _copy_kernel, grid=(T // 512,),
        in_specs=[pl.BlockSpec((512, D), lambda i: (i, 0))],
        out_specs=pl.BlockSpec((512, D), lambda i: (i, 0)),
        out_shape=jax.ShapeDtypeStruct(x.shape, x.dtype))(x)


def kernel(x_prompt, x_sample, c, state_rwkv_fwd, state_rwkv_bwd, c_ctx, w_mod, b_mod, ln_mix_g, ln_mix_b, ln_ffn_g, ln_ffn_b, peer_wq, peer_k1, peer_k2, peer_u, peer_v, ab_w_in, ab_mu, rwkv_w0, rwkv_w2, rwkv_a0, rwkv_a2, rwkv_g2, rwkv_k_k, rwkv_k_a, rwkv_r_k, rwkv_gn_g, rwkv_gn_b, ab_w_out, hy_w_in, hy_conv_w, hy_conv_b, hy_filt_w1, hy_filt_b1, hy_filt_w2, hy_filt_b2, hy_filt_w3, hy_filt_freq, hy_filt_decay, hy_skip, hy_w_out):
    P = {
        'w_mod': w_mod, 'b_mod': b_mod,
        'ln_mix_g': ln_mix_g, 'ln_mix_b': ln_mix_b, 'ln_ffn_g': ln_ffn_g, 'ln_ffn_b': ln_ffn_b,
        'peer_wq': peer_wq, 'peer_k1': peer_k1, 'peer_k2': peer_k2, 'peer_u': peer_u, 'peer_v': peer_v,
        'ab_w_in': ab_w_in, 'ab_mu': ab_mu, 'rwkv_w0': rwkv_w0, 'rwkv_w2': rwkv_w2,
        'rwkv_a0': rwkv_a0, 'rwkv_a2': rwkv_a2, 'rwkv_g2': rwkv_g2, 'rwkv_k_k': rwkv_k_k,
        'rwkv_k_a': rwkv_k_a, 'rwkv_r_k': rwkv_r_k, 'rwkv_gn_g': rwkv_gn_g, 'rwkv_gn_b': rwkv_gn_b,
        'ab_w_out': ab_w_out,
        'hy_w_in': hy_w_in, 'hy_conv_w': hy_conv_w, 'hy_conv_b': hy_conv_b,
        'hy_filt_w1': hy_filt_w1, 'hy_filt_b1': hy_filt_b1, 'hy_filt_w2': hy_filt_w2,
        'hy_filt_b2': hy_filt_b2, 'hy_filt_w3': hy_filt_w3, 'hy_filt_freq': hy_filt_freq,
        'hy_filt_decay': hy_filt_decay, 'hy_skip': hy_skip, 'hy_w_out': hy_w_out,
    }
    x_ctx = x_prompt
    c_ctx_vec = c_ctx[None, None, :]
    zero_state = jnp.zeros((x_prompt.shape[0], RWKV_HEADS, HEAD_SIZE, HEAD_SIZE), F32)
    x_lat = x_sample + grid_pos_embed(x_sample.shape[1]).astype(x_sample.dtype)
    c_lat_vec = c[:, None, :]
    new_f, new_b = [], []
    for layer in range(DEPTH):
        x_ctx, s_f, s_b = run_layer(layer, x_ctx, c_ctx_vec, zero_state, zero_state, P)
        if layer % 2 == 0:
            e = layer // 2
            new_f.append(s_f)
            new_b.append(s_b)
            x_lat, _, _ = run_layer(layer, x_lat, c_lat_vec,
                                    state_rwkv_fwd[:, e], state_rwkv_bwd[:, e], P)
        else:
            x_lat, _, _ = run_layer(layer, x_lat, c_lat_vec, None, None, P)
    y_prompt = _pallas_copy(x_ctx.reshape(-1, D_MODEL)).reshape(x_ctx.shape)
    return (y_prompt, x_lat, jnp.stack(new_f, axis=1), jnp.stack(new_b, axis=1))
```

```python
import functools
import math
import jax
import jax.numpy as jnp
from jax import lax
from jax.experimental import pallas as pl
from jax.experimental.pallas import tpu as pltpu

D_MODEL = 1024
BATCH = 32
SEQ = 256
DEPTH = 2
DEC_BATCH = 8
DEC_SEQ = 2048
GRID_W = 64
N_EVEN = (DEPTH + 1) // 2
N_ODD = DEPTH // 2
RWKV_HEADS = 12
HEAD_SIZE = 64
D_RWKV = RWKV_HEADS * HEAD_SIZE
DECAY_LORA = 64
AAA_LORA = 64
GATE_LORA = 128
GN_EPS = 64e-5
FNET_GROUPS = 4
FNET_GROUP_W = 64
D_FNET = FNET_GROUPS * FNET_GROUP_W
D_SHIFTED = 3 * D_RWKV + 2 * DECAY_LORA + 2 * AAA_LORA + GATE_LORA
D_AB_IN = D_SHIFTED + D_FNET
HY_BANDS = 16
HY_EMB = 2 * HY_BANDS + 1
HY_ORDER = 64
PEER_HEADS = 8
N_KEYS = 128
N_EXPERTS = N_KEYS * N_KEYS
PEER_DK = 256
PEER_TOPK = 16
PEER_BLOCK = 128
LN_EPS = 1e-5
DN_ALPHA = (2 * DEPTH) ** 0.25
DN_BETA = (8 * DEPTH) ** -0.25
F32 = jnp.float32


def layer_norm(x, g=None, b=None):
    xf = x.astype(F32)
    mu = jnp.mean(xf, axis=-1, keepdims=True)
    var = jnp.mean(jnp.square(xf - mu), axis=-1, keepdims=True)
    y = (xf - mu) * lax.rsqrt(var + LN_EPS)
    if g is not None:
        y = y * g.astype(F32) + b.astype(F32)
    return y.astype(x.dtype)


def shift_prev(z):
    return jnp.pad(z, ((0, 0), (1, 0), (0, 0)))[:, :-1]


def shift_next(z):
    return jnp.pad(z, ((0, 0), (0, 1), (0, 0)))[:, 1:]


def modulation(cvec, w_mod, b_mod):
    m = jax.nn.silu(cvec) @ w_mod + b_mod
    return jnp.split(m, 6, axis=-1)


def grid_pos_embed(n_tokens):
    rows = n_tokens // GRID_W
    row = jnp.repeat(jnp.arange(rows, dtype=F32), GRID_W)
    col = jnp.tile(jnp.arange(GRID_W, dtype=F32), rows)
    quarter = D_MODEL // 4
    omega = 1.0 / (10000.0 ** (jnp.arange(quarter, dtype=F32) / quarter))
    ra = row[:, None] * omega[None, :]
    ca = col[:, None] * omega[None, :]
    return jnp.concatenate([jnp.sin(ra), jnp.cos(ra), jnp.sin(ca), jnp.cos(ca)], axis=-1)


BF16 = jnp.bfloat16
VMEM_LIMIT_BYTES = 56 * 1024 * 1024
LANES = 128
WKV_CHUNK = 16
WKV_UNROLL = 4


def _split(x):
    hi = x.astype(BF16)
    return hi, (x - hi.astype(F32)).astype(BF16)


def _dot3(a, b, dims=(((1,), (0,)), ((), ()))):
    ah, al = _split(a)
    bh, bl = _split(b)
    mm = functools.partial(lax.dot_general, dimension_numbers=dims, preferred_element_type=F32)
    return mm(ah, bh) + (mm(al, bh) + mm(ah, bl))


_NT = (((1,), (1,)), ((), ()))


def _pair_rows(z, lo):
    return jnp.concatenate([jnp.where(lo, z, 0.0), jnp.where(lo, 0.0, z)], axis=0)


def _chunk_transition(lw, ak, kka, k, r, v, rev):
    C = WKV_CHUNK
    t_i = lax.broadcasted_iota(jnp.int32, (C, C), 0)
    s_i = lax.broadcasted_iota(jnp.int32, (C, C), 1)
    upto = (s_i >= t_i) if rev else (s_i <= t_i)
    cum = _dot3(jnp.where(upto, 1.0, 0.0), lw)
    yield
    tot = cum[0:1] if rev else cum[C - 1:C]
    a_t = ak * jnp.exp(cum - lw)
    r_t = r * jnp.exp(cum)
    inv = jnp.exp(-cum)
    b_t, k_t = kka * inv, k * inv
    end = jnp.exp(tot - cum)
    b_e, k_e = kka * end, k * end
    lo = lax.broadcasted_iota(jnp.int32, (1, LANES), 1) < HEAD_SIZE
    lo2 = jnp.concatenate([lo, lo], axis=1)
    t2 = lax.broadcasted_iota(jnp.int32, (C, 2 * C), 0)
    s2 = lax.broadcasted_iota(jnp.int32, (C, 2 * C), 1) % C
    before2 = (s2 > t2) if rev else (s2 < t2)
    upto2 = (s2 >= t2) if rev else (s2 <= t2)
    x = jnp.concatenate([a_t, r_t], axis=0)
    gb = _dot3(x, _pair_rows(b_t, lo), _NT)
    gk = _dot3(x, _pair_rows(k_t, lo), _NT)
    nt = _dot3(b_t, _pair_rows(a_t, lo), _NT)
    yield
    lak = jnp.where(before2, gk[0:C], 0.0)
    lrb = jnp.where(upto2, gb[C:2 * C], 0.0)
    lrk = jnp.where(upto2, gk[C:2 * C], 0.0)
    nt = jnp.where((t2 > s2) if rev else (t2 < s2), nt, 0.0)
    lane2 = lax.broadcasted_iota(jnp.int32, (C, 2 * C), 1)
    row2 = t2
    lakv = _dot3(lak, _pair_rows(v, lo))
    y1 = _dot3(lrk, _pair_rows(v, lo))
    tinv = jnp.zeros((C, 2 * C), F32)
    for t in (range(C - 1, -1, -1) if rev else range(C)):
        coef = jnp.take_along_axis(nt, t + C * (lane2 // C), axis=1)
        new = jnp.where(lane2 % C == t, 1.0, 0.0)[0:1] + jnp.sum(coef * tinv, axis=0, keepdims=True)
        tinv = jnp.where(row2 == t, new, tinv)
        yield
    m1w2 = _dot3(tinv, _pair_rows(jnp.concatenate([a_t, lakv], axis=1), lo2))
    yield
    my = _dot3(lrb, _pair_rows(m1w2, lo2))
    z = jnp.concatenate([b_e, k_e], axis=0)
    rhs = jnp.concatenate([m1w2, jnp.concatenate([jnp.zeros_like(v), v], axis=1)], axis=0)
    p = _dot3(z.T, rhs)
    yield
    m2 = r_t + my[:, :LANES]
    y1 = my[:, LANES:] + y1
    r_l = lax.broadcasted_iota(jnp.int32, (LANES, LANES), 0)
    c_l = lax.broadcasted_iota(jnp.int32, (LANES, LANES), 1)
    same = (r_l < HEAD_SIZE) == (c_l < HEAD_SIZE)
    m3 = jnp.where(same, p[:, :LANES], 0.0) + jnp.where(r_l == c_l, jnp.exp(tot), 0.0)
    s1 = jnp.where(same, p[:, LANES:], 0.0)
    yield m2, y1, m3, s1


def _lockstep(gens):
    out = [None] * len(gens)
    live = list(range(len(gens)))
    while live:
        for i in list(live):
            try:
                val = next(gens[i])
            except StopIteration:
                live.remove(i)
            else:
                out[i] = val if val is not None else out[i]
    return out


def _wkv_kernel(ak_ref, r_ref, v_ref, lwf_ref, kkaf_ref, kf_ref, lwb_ref, kkab_ref, kb_ref, s0f_ref, s0b_ref,
                y_ref, sf_ref, sb_ref, yb_ref):
    L = v_ref.shape[1]
    C = WKV_CHUNK
    n = L // C

    def load(ref, o):
        return ref[0, pl.ds(o, C), :]

    def body(c, carry):
        sf, sb = carry
        offs_f = [pl.multiple_of((c * WKV_UNROLL + i) * C, C) for i in range(WKV_UNROLL)]
        offs_b = [pl.multiple_of((n - 1 - c * WKV_UNROLL - i) * C, C) for i in range(WKV_UNROLL)]
        gf = [_chunk_transition(load(lwf_ref, o), load(ak_ref, o), load(kkaf_ref, o), load(kf_ref, o),
                                load(r_ref, o), load(v_ref, o), False) for o in offs_f]
        gb = [_chunk_transition(load(lwb_ref, o), load(ak_ref, o), load(kkab_ref, o), load(kb_ref, o),
                                load(r_ref, o), load(v_ref, o), True) for o in offs_b]
        done = _lockstep(gf + gb)
        tf, tb = done[:WKV_UNROLL], done[WKV_UNROLL:]
        for i in range(WKV_UNROLL):
            m2, y1, m3, s1 = tf[i]
            y_ref[0, pl.ds(offs_f[i], C), :] = _dot3(m2, sf) + y1
            sf = _dot3(m3, sf) + s1
            m2, y1, m3, s1 = tb[i]
            yb_ref[pl.ds(offs_b[i], C), :] = _dot3(m2, sb) + y1
            sb = _dot3(m3, sb) + s1
        return sf, sb

    sf, sb = lax.fori_loop(0, n // WKV_UNROLL, body, (s0f_ref[0, 0], s0b_ref[0, 0]))
    sf_ref[0, 0] = sf
    sb_ref[0, 0] = sb
    y_ref[0] = y_ref[0] + yb_ref[...]


def wkv_pallas(ak, r, v, lwf, kkaf, kf, lwb, kkab, kb, s0f, s0b):
    B, L, Cd = v.shape
    npair = Cd // LANES
    seq = pl.BlockSpec((1, L, LANES), lambda b, p: (b, 0, p))
    st = pl.BlockSpec((1, 1, LANES, LANES), lambda b, p: (b, p, 0, 0))
    st_shape = jax.ShapeDtypeStruct((B, npair, LANES, LANES), F32)
    return pl.pallas_call(
        _wkv_kernel, grid=(B, npair),
        in_specs=[seq] * 9 + [st, st],
        out_specs=[seq, st, st],
        out_shape=[jax.ShapeDtypeStruct((B, L, Cd), F32), st_shape, st_shape],
        scratch_shapes=[pltpu.VMEM((L, LANES), F32)],
        compiler_params=pltpu.CompilerParams(dimension_semantics=("arbitrary", "arbitrary"),
                                             vmem_limit_bytes=VMEM_LIMIT_BYTES),
        name="wkv_scan",
    )(ak, r, v, lwf, kkaf, kf, lwb, kkab, kb, s0f, s0b)


def _pack_state(s):
    B = s.shape[0]
    st = jnp.swapaxes(s, -1, -2).reshape(B, RWKV_HEADS // 2, 2, HEAD_SIZE, HEAD_SIZE)
    z = jnp.zeros_like(st[:, :, 0])
    top = jnp.concatenate([st[:, :, 0], z], axis=-1)
    bot = jnp.concatenate([z, st[:, :, 1]], axis=-1)
    return jnp.concatenate([top, bot], axis=-2)


def _unpack_state(sd):
    B = sd.shape[0]
    a = sd[:, :, :HEAD_SIZE, :HEAD_SIZE]
    b = sd[:, :, HEAD_SIZE:, HEAD_SIZE:]
    st = jnp.stack([a, b], axis=2).reshape(B, RWKV_HEADS, HEAD_SIZE, HEAD_SIZE)
    return jnp.swapaxes(st, -1, -2)


def rwkv_mix(zs, e, P, s0_f, s0_b):
    B, L, _ = zs.shape
    splits = [D_RWKV, 2 * D_RWKV, 3 * D_RWKV,
              3 * D_RWKV + DECAY_LORA, 3 * D_RWKV + 2 * DECAY_LORA,
              3 * D_RWKV + 2 * DECAY_LORA + AAA_LORA, 3 * D_RWKV + 2 * DECAY_LORA + 2 * AAA_LORA]
    r, k, v, zw_f, zw_b, za_f, za_b, zg = jnp.split(zs, splits, axis=-1)

    def heads(t):
        return t.astype(F32).reshape(B, L, RWKV_HEADS, HEAD_SIZE)

    r_h, k_h, v_h = heads(r), heads(k), heads(v)
    kk = heads(k * P['rwkv_k_k'][e])
    kk = kk * lax.rsqrt(jnp.sum(kk * kk, axis=-1, keepdims=True) + 1e-12)
    k_a = P['rwkv_k_a'][e].astype(F32).reshape(RWKV_HEADS, HEAD_SIZE)
    k_dirs, ops = [], []
    flat = lambda t: t.reshape(B, L, D_RWKV)
    for d, (zw, za) in enumerate(((zw_f, za_f), (zw_b, za_b))):
        w_raw = (P['rwkv_w0'][e, d] + jnp.tanh(zw) @ P['rwkv_w2'][e, d]).astype(F32)
        log_decay = -jnp.exp(-jax.nn.softplus(-w_raw) - 0.5)
        a = heads(jax.nn.sigmoid((P['rwkv_a0'][e, d] + za @ P['rwkv_a2'][e, d]).astype(F32)))
        k_d = k_h * (1.0 + (a - 1.0) * k_a)
        ops += [log_decay, flat(kk * a), flat(k_d)]
        k_dirs.append(k_d)
    y, sd_f, sd_b = wkv_pallas(flat(-kk), flat(r_h), flat(v_h), *ops,
                               _pack_state(s0_f.astype(F32)), _pack_state(s0_b.astype(F32)))
    finals = [_unpack_state(sd_f), _unpack_state(sd_b)]
    y = y.reshape(B, L, RWKV_HEADS, HEAD_SIZE)
    mu = jnp.mean(y, axis=-1, keepdims=True)
    var = jnp.mean(jnp.square(y - mu), axis=-1, keepdims=True)
    y = ((y - mu) * lax.rsqrt(var + GN_EPS)).reshape(B, L, D_RWKV)
    y = y * P['rwkv_gn_g'][e].astype(F32) + P['rwkv_gn_b'][e].astype(F32)
    k_bonus = 0.5 * (k_dirs[0] + k_dirs[1])
    bonus = jnp.sum(r_h * k_bonus * P['rwkv_r_k'][e].astype(F32), axis=-1, keepdims=True) * v_h
    y = y + bonus.reshape(B, L, D_RWKV)
    g = jax.nn.sigmoid(zg) @ P['rwkv_g2'][e]
    return y.astype(zs.dtype) * g, finals[0], finals[1]


def fourier_mix(zf):
    B, L, _ = zf.shape
    z = zf.astype(F32).reshape(B, L, FNET_GROUPS, FNET_GROUP_W)
    y = jnp.fft.fft2(z, axes=(1, 3), norm='ortho').real
    return y.reshape(B, L, D_FNET).astype(zf.dtype)


def ab_mixer(z, e, P, s0_f, s0_b):
    zs, zf = z[..., :D_SHIFTED], z[..., D_SHIFTED:]
    zs = zs + P['ab_mu'][e] * (0.5 * (shift_prev(zs) + shift_next(zs)) - zs)
    y_a, s_f, s_b = rwkv_mix(zs, e, P, s0_f, s0_b)
    y_b = fourier_mix(zf)
    return y_a, y_b, s_f, s_b


def hyena_filters(L, w1, b1, w2, b2, w3, freq, decay):
    pos = jnp.arange(L, dtype=F32)[:, None]
    t = pos / (L - 1)
    bands = jnp.linspace(1e-4, HY_BANDS - 1, HY_BANDS, dtype=F32)[None, :]
    ang = 2.0 * math.pi * pos / L * bands
    feats = jnp.concatenate([t, jnp.cos(ang), -jnp.sin(ang)], axis=-1)
    fr = freq.astype(F32)
    hid = jnp.sin(fr * (feats @ w1.astype(F32) + b1.astype(F32)))
    hid = jnp.sin(fr * (hid @ w2.astype(F32) + b2.astype(F32)))
    filt = (hid @ w3.astype(F32)) * jnp.exp(-t * jnp.abs(decay.astype(F32)))
    return filt[:, :D_MODEL], filt[:, D_MODEL:]


def long_conv(u, h_f, h_b, skip):
    L, D = h_f.shape
    taps = jnp.concatenate([h_f, jnp.zeros((1, D), F32), h_b[:0:-1]], axis=0)
    uf = jnp.fft.rfft(u, n=2 * L, axis=1)
    tf = jnp.fft.rfft(taps, axis=0)
    y = jnp.fft.irfft(uf * tf[None], n=2 * L, axis=1)[:, :L]
    return y + u * skip


def hyena_mixer(z, o, P):
    L = z.shape[1]
    cw = P['hy_conv_w'][o]
    z = cw[0] * shift_prev(z) + cw[1] * z + cw[2] * shift_next(z) + P['hy_conv_b'][o]
    x0, x1, v = jnp.split(z, 3, axis=-1)
    h_f, h_b = hyena_filters(L, P['hy_filt_w1'][o], P['hy_filt_b1'][o], P['hy_filt_w2'][o],
                             P['hy_filt_b2'][o], P['hy_filt_w3'][o], P['hy_filt_freq'][o],
                             P['hy_filt_decay'][o])
    y = long_conv((v * x1).astype(F32), h_f, h_b, P['hy_skip'][o].astype(F32))
    return y * x0


NEG_INF = float('-inf')
ROUTE_TT = 256
EXPERT_TT = 512
EXPERT_ET = 512


def _top_rows(x, n):
    row = lax.broadcasted_iota(jnp.int32, (n, x.shape[1]), 0)
    out = jnp.zeros((n, x.shape[1]), F32)
    for j in range(n):
        m = jnp.max(x, axis=0, keepdims=True)
        out = jnp.where(row == j, m, out)
        x = jnp.where(x == m, NEG_INF, x)
    return out


def _route_kernel(xT_ref, wqT_ref, k1_ref, k2_ref, s1_ref, s2_ref, p1_ref, p2_ref, tau_ref, qT_ref):
    qT_ref[...] = jnp.dot(wqT_ref[...], xT_ref[...], preferred_element_type=F32)
    half = PEER_DK // 2
    lanes = 128

    def head(h, carry):
        base = pl.multiple_of(h * PEER_DK, PEER_DK)
        q1 = qT_ref[pl.ds(base, half), :].astype(BF16)
        q2 = qT_ref[pl.ds(base + half, half), :].astype(BF16)
        s1 = jnp.dot(k1_ref[h], q1, preferred_element_type=F32)
        s2 = jnp.dot(k2_ref[h], q2, preferred_element_type=F32)
        s1_ref[h] = s1
        s2_ref[h] = s2
        for c in range(s1.shape[1] // lanes):
            sl = slice(c * lanes, (c + 1) * lanes)
            s1c, s2c = s1[:, sl], s2[:, sl]
            a = _top_rows(s1c, PEER_TOPK)
            b = _top_rows(s2c, PEER_TOPK)
            parts = [a[0:1] + b]
            for i in range(1, 8):
                parts.append(a[i:i + 1] + b[0:8])
            parts.append(a[8:16] + b[0:1])
            cand = jnp.concatenate(parts, axis=0)
            top = cand[0:1]
            work = cand
            tau = jnp.zeros_like(top)
            left = jnp.full(top.shape, float(PEER_TOPK), F32)
            for _ in range(PEER_TOPK):
                m = jnp.max(work, axis=0, keepdims=True)
                hit = work == m
                tau = jnp.where(left > 0.0, m, tau)
                left = left - jnp.sum(jnp.where(hit, 1.0, 0.0), axis=0, keepdims=True)
                work = jnp.where(hit, NEG_INF, work)
            z = jnp.sum(jnp.where(cand >= tau, jnp.exp(cand - top), 0.0), axis=0, keepdims=True)
            tau_ref[h, :, sl] = tau
            p1_ref[h, :, sl] = jnp.exp(s1c - a[0:1]) / z
            p2_ref[h, :, sl] = jnp.exp(s2c - b[0:1])
        return carry

    lax.fori_loop(0, PEER_HEADS, head, 0)


def _expert_kernel(xT_ref, s1_ref, s2_ref, p1_ref, p2_ref, tau_ref, u_ref, vT_ref, o_ref, w_ref):
    i = pl.program_id(1)
    last = pl.num_programs(1) - 2

    @pl.when(i == 0)
    def _():
        o_ref[...] = jnp.zeros_like(o_ref)
        w_ref[1] = jnp.zeros(w_ref.shape[1:], BF16)

    slot = i % 2
    n1 = u_ref.shape[0] // N_KEYS
    o_ref[...] += jnp.dot(vT_ref[...], w_ref[1 - slot], preferred_element_type=F32)
    pre = jnp.dot(u_ref[...], xT_ref[...], preferred_element_type=F32)
    for g in range(n1):
        rows = slice(g * N_KEYS, (g + 1) * N_KEYS)
        i1 = jnp.minimum(i, last) * n1 + g
        gate = jnp.zeros((N_KEYS, xT_ref.shape[1]), F32)
        for h in range(PEER_HEADS):
            s = s1_ref[h, pl.ds(i1, 1), :] + s2_ref[h]
            p = p1_ref[h, pl.ds(i1, 1), :] * p2_ref[h]
            gate = gate + jnp.where(s >= tau_ref[h], p, 0.0)
        w_ref[slot, rows, :] = (jax.nn.gelu(pre[rows]) * gate).astype(BF16)


def peer_pallas(hT, wqT, k1, k2, u, vT):
    D, T = hT.shape
    H, NK = PEER_HEADS, N_KEYS
    route_shapes = [jax.ShapeDtypeStruct((H, NK, T), F32)] * 4 + [jax.ShapeDtypeStruct((H, 1, T), F32)]
    tt = ROUTE_TT
    s1, s2, p1, p2, tau = pl.pallas_call(
        _route_kernel,
        grid=(T // tt,),
        in_specs=[pl.BlockSpec((D, tt), lambda j: (0, j)),
                  pl.BlockSpec((H * PEER_DK, D), lambda j: (0, 0)),
                  pl.BlockSpec((H, NK, PEER_DK // 2), lambda j: (0, 0, 0)),
                  pl.BlockSpec((H, NK, PEER_DK // 2), lambda j: (0, 0, 0))],
        out_specs=[pl.BlockSpec((H, NK, tt), lambda j: (0, 0, j))] * 4 + [pl.BlockSpec((H, 1, tt), lambda j: (0, 0, j))],
        out_shape=route_shapes,
        scratch_shapes=[pltpu.VMEM((H * PEER_DK, tt), F32)],
        compiler_params=pltpu.CompilerParams(dimension_semantics=("arbitrary",),
                                             vmem_limit_bytes=VMEM_LIMIT_BYTES),
        name="peer_route",
    )(hT, wqT, k1, k2)
    tt, et = EXPERT_TT, EXPERT_ET
    route_spec = pl.BlockSpec((H, NK, tt), lambda j, i: (0, 0, j))
    n_et = N_EXPERTS // et
    return pl.pallas_call(
        _expert_kernel,
        grid=(T // tt, n_et + 1),
        in_specs=[pl.BlockSpec((D, tt), lambda j, i: (0, j)),
                  route_spec, route_spec, route_spec, route_spec,
                  pl.BlockSpec((H, 1, tt), lambda j, i: (0, 0, j)),
                  pl.BlockSpec((et, D), lambda j, i: (jnp.minimum(i, n_et - 1), 0)),
                  pl.BlockSpec((D, et), lambda j, i: (0, jnp.maximum(i - 1, 0)))],
        out_specs=pl.BlockSpec((D, tt), lambda j, i: (0, j)),
        out_shape=jax.ShapeDtypeStruct((D, T), F32),
        scratch_shapes=[pltpu.VMEM((2, et, tt), BF16)],
        compiler_params=pltpu.CompilerParams(dimension_semantics=("arbitrary", "arbitrary"),
                                             vmem_limit_bytes=VMEM_LIMIT_BYTES),
        name="peer_experts",
    )(hT, s1, s2, p1, p2, tau, u, vT)


TOKEN_TM = 512
MOD_ROWS = 6


def _ln_rows(x):
    mu = jnp.mean(x, axis=-1, keepdims=True)
    xc = x - mu
    return xc * lax.rsqrt(jnp.mean(xc * xc, axis=-1, keepdims=True) + LN_EPS)


def _mod_kernel(c_ref, w_ref, b_ref, o_ref):
    act = jax.nn.silu(c_ref[...]).astype(BF16)
    o_ref[...] = jnp.dot(act, w_ref[...].astype(BF16), preferred_element_type=F32) + b_ref[...]


def modulation_pallas(cvecs, w_mod, b_mod):
    R, D = cvecs.shape
    N = w_mod.shape[1]
    out = pl.pallas_call(
        _mod_kernel, grid=(N // D,),
        in_specs=[pl.BlockSpec((R, D), lambda n: (0, 0)),
                  pl.BlockSpec((D, D), lambda n: (0, n)),
                  pl.BlockSpec((1, D), lambda n: (0, n))],
        out_specs=pl.BlockSpec((R, D), lambda n: (0, n)),
        out_shape=jax.ShapeDtypeStruct((R, N), F32),
        name="modulation",
    )(cvecs, w_mod, b_mod.reshape(1, N))
    return out.reshape(R, MOD_ROWS, D)


def _mod_spec(tm, rows_per_mod, mod_off):
    return pl.BlockSpec((1, MOD_ROWS, D_MODEL), lambda i: (mod_off + (i * tm) // rows_per_mod, 0, 0))


def _token_params():
    return pltpu.CompilerParams(dimension_semantics=("arbitrary",), vmem_limit_bytes=VMEM_LIMIT_BYTES)


def _lnmm_kernel(x_ref, mod_ref, w_ref, o_ref):
    h = _ln_rows(x_ref[...]) * (1.0 + mod_ref[0, 1:2, :]) + mod_ref[0, 0:1, :]
    o_ref[...] = jnp.dot(h.astype(BF16), w_ref[...], preferred_element_type=F32)


def ln_mod_matmul(x, mods, rows_per_mod, mod_off, w):
    T, D = x.shape
    N = w.shape[1]
    tm = TOKEN_TM
    return pl.pallas_call(
        _lnmm_kernel, grid=(T // tm,),
        in_specs=[pl.BlockSpec((tm, D), lambda i: (i, 0)),
                  _mod_spec(tm, rows_per_mod, mod_off),
                  pl.BlockSpec((D, N), lambda i: (0, 0))],
        out_specs=pl.BlockSpec((tm, N), lambda i: (i, 0)),
        out_shape=jax.ShapeDtypeStruct((T, N), F32),
        compiler_params=_token_params(),
        name="ln_mod_matmul",
    )(x, mods, w)


def _proj_ln_kernel(*refs, n_in):
    a_refs, w_refs = refs[:n_in], refs[n_in:2 * n_in]
    x_ref, mod_ref, ln_ref, xo_ref, hT_ref = refs[2 * n_in:]
    y = jnp.dot(a_refs[0][...].astype(BF16), w_refs[0][...], preferred_element_type=F32)
    for a_ref, w_ref in zip(a_refs[1:], w_refs[1:]):
        y = y + jnp.dot(a_ref[...].astype(BF16), w_ref[...], preferred_element_type=F32)
    x = _ln_rows(DN_ALPHA * x_ref[...] + mod_ref[0, 2:3, :] * y) * ln_ref[0:1, :] + ln_ref[1:2, :]
    xo_ref[...] = x
    h = _ln_rows(x) * (1.0 + mod_ref[0, 4:5, :]) + mod_ref[0, 3:4, :]
    hT_ref[...] = h.T.astype(BF16)


def proj_ln(acts, ws, x, mods, rows_per_mod, mod_off, ln):
    T, D = x.shape
    tm = TOKEN_TM
    return pl.pallas_call(
        functools.partial(_proj_ln_kernel, n_in=len(acts)), grid=(T // tm,),
        in_specs=[pl.BlockSpec((tm, a.shape[1]), lambda i: (i, 0)) for a in acts]
        + [pl.BlockSpec(w.shape, lambda i: (0, 0)) for w in ws]
        + [pl.BlockSpec((tm, D), lambda i: (i, 0)), _mod_spec(tm, rows_per_mod, mod_off),
           pl.BlockSpec((2, D), lambda i: (0, 0))],
        out_specs=[pl.BlockSpec((tm, D), lambda i: (i, 0)), pl.BlockSpec((D, tm), lambda i: (0, i))],
        out_shape=[jax.ShapeDtypeStruct((T, D), F32), jax.ShapeDtypeStruct((D, T), BF16)],
        compiler_params=_token_params(),
        name="proj_ln",
    )(*acts, *ws, x, mods, ln)


def _peer_out_kernel(yT_ref, x_ref, mod_ref, ln_ref, xo_ref):
    y = yT_ref[...].T
    xo_ref[...] = _ln_rows(DN_ALPHA * x_ref[...] + mod_ref[0, 5:6, :] * y) * ln_ref[0:1, :] + ln_ref[1:2, :]


def peer_out_ln(yT, tile_off, x, mods, rows_per_mod, mod_off, ln):
    T, D = x.shape
    tm = TOKEN_TM
    return pl.pallas_call(
        _peer_out_kernel, grid=(T // tm,),
        in_specs=[pl.BlockSpec((D, tm), lambda i: (0, tile_off + i)),
                  pl.BlockSpec((tm, D), lambda i: (i, 0)), _mod_spec(tm, rows_per_mod, mod_off),
                  pl.BlockSpec((2, D), lambda i: (0, 0))],
        out_specs=pl.BlockSpec((tm, D), lambda i: (i, 0)),
        out_shape=jax.ShapeDtypeStruct((T, D), F32),
        compiler_params=_token_params(),
        name="peer_out_ln",
    )(yT, x, mods, ln)


def kernel(x_prompt, x_sample, c, state_rwkv_fwd, state_rwkv_bwd, c_ctx, w_mod, b_mod, ln_mix_g, ln_mix_b, ln_ffn_g, ln_ffn_b, peer_wq, peer_k1, peer_k2, peer_u, peer_v, ab_w_in, ab_mu, rwkv_w0, rwkv_w2, rwkv_a0, rwkv_a2, rwkv_g2, rwkv_k_k, rwkv_k_a, rwkv_r_k, rwkv_gn_g, rwkv_gn_b, ab_w_out, hy_w_in, hy_conv_w, hy_conv_b, hy_filt_w1, hy_filt_b1, hy_filt_w2, hy_filt_b2, hy_filt_w3, hy_filt_freq, hy_filt_decay, hy_skip, hy_w_out):
    P = {
        'w_mod': w_mod, 'b_mod': b_mod,
        'ln_mix_g': ln_mix_g, 'ln_mix_b': ln_mix_b, 'ln_ffn_g': ln_ffn_g, 'ln_ffn_b': ln_ffn_b,
        'peer_wq': peer_wq, 'peer_k1': peer_k1, 'peer_k2': peer_k2, 'peer_u': peer_u, 'peer_v': peer_v,
        'ab_w_in': ab_w_in, 'ab_mu': ab_mu, 'rwkv_w0': rwkv_w0, 'rwkv_w2': rwkv_w2,
        'rwkv_a0': rwkv_a0, 'rwkv_a2': rwkv_a2, 'rwkv_g2': rwkv_g2, 'rwkv_k_k': rwkv_k_k,
        'rwkv_k_a': rwkv_k_a, 'rwkv_r_k': rwkv_r_k, 'rwkv_gn_g': rwkv_gn_g, 'rwkv_gn_b': rwkv_gn_b,
        'ab_w_out': ab_w_out,
        'hy_w_in': hy_w_in, 'hy_conv_w': hy_conv_w, 'hy_conv_b': hy_conv_b,
        'hy_filt_w1': hy_filt_w1, 'hy_filt_b1': hy_filt_b1, 'hy_filt_w2': hy_filt_w2,
        'hy_filt_b2': hy_filt_b2, 'hy_filt_w3': hy_filt_w3, 'hy_filt_freq': hy_filt_freq,
        'hy_filt_decay': hy_filt_decay, 'hy_skip': hy_skip, 'hy_w_out': hy_w_out,
    }
    D = D_MODEL
    zero_state = jnp.zeros((BATCH, RWKV_HEADS, HEAD_SIZE, HEAD_SIZE), F32)
    x_lat = x_sample + grid_pos_embed(DEC_SEQ).astype(x_sample.dtype)
    cvecs = jnp.concatenate([c_ctx[None, :], c], axis=0)
    groups = [dict(x=x_prompt.reshape(-1, D), B=BATCH, L=SEQ, per_mod=BATCH * SEQ, mod_off=0),
              dict(x=x_lat.reshape(-1, D), B=DEC_BATCH, L=DEC_SEQ, per_mod=DEC_SEQ, mod_off=1)]
    new_f, new_b = [], []
    for layer in range(DEPTH):
        mods = modulation_pallas(cvecs, w_mod[layer], b_mod[layer])
        ln_mix = jnp.stack([ln_mix_g[layer], ln_mix_b[layer]])
        ln_ffn = jnp.stack([ln_ffn_g[layer], ln_ffn_b[layer]])
        e = layer // 2
        hts = []
        for gi, g in enumerate(groups):
            mod_args = (mods, g['per_mod'], g['mod_off'])
            if layer % 2 == 0:
                z = ln_mod_matmul(g['x'], *mod_args, ab_w_in[e].astype(BF16)).reshape(g['B'], g['L'], D_AB_IN)
                s0 = (zero_state, zero_state) if gi == 0 else (state_rwkv_fwd[:, e], state_rwkv_bwd[:, e])
                y_a, y_b, s_f, s_b = ab_mixer(z, e, P, *s0)
                if gi == 0:
                    new_f.append(s_f)
                    new_b.append(s_b)
                w_out = ab_w_out[e].astype(BF16)
                acts = [y_a.reshape(-1, D_RWKV), y_b.reshape(-1, D_FNET)]
                ws = [w_out[:D_RWKV], w_out[D_RWKV:]]
            else:
                z = ln_mod_matmul(g['x'], *mod_args, hy_w_in[e].astype(BF16)).reshape(g['B'], g['L'], 3 * D)
                acts = [hyena_mixer(z, e, P).reshape(-1, D)]
                ws = [hy_w_out[e].astype(BF16)]
            g['x'], ht = proj_ln(acts, ws, g['x'], *mod_args, ln_mix)
            hts.append(ht)
        yT = peer_pallas(jnp.concatenate(hts, axis=1), peer_wq[layer].T.astype(BF16),
                         peer_k1[layer].astype(BF16), peer_k2[layer].astype(BF16),
                         peer_u[layer].astype(BF16), peer_v[layer].T.astype(BF16))
        tile_off = 0
        for g in groups:
            g['x'] = peer_out_ln(yT, tile_off, g['x'], mods, g['per_mod'], g['mod_off'], ln_ffn)
            tile_off += g['x'].shape[0] // TOKEN_TM
    y_prompt = groups[0]['x'].reshape(BATCH, SEQ, D)
    y_sample = groups[1]['x'].reshape(DEC_BATCH, DEC_SEQ, D)
    return (y_prompt, y_sample, jnp.stack(new_f, axis=1), jnp.stack(new_b, axis=1))
```

```python
import functools
import math
import jax
import jax.numpy as jnp
from jax import lax
from jax.experimental import pallas as pl
from jax.experimental.pallas import tpu as pltpu

D_MODEL = 1024
BATCH = 32
SEQ = 256
DEPTH = 2
DEC_BATCH = 8
DEC_SEQ = 2048
GRID_W = 64
N_EVEN = (DEPTH + 1) // 2
N_ODD = DEPTH // 2
RWKV_HEADS = 12
HEAD_SIZE = 64
D_RWKV = RWKV_HEADS * HEAD_SIZE
DECAY_LORA = 64
AAA_LORA = 64
GATE_LORA = 128
GN_EPS = 64e-5
FNET_GROUPS = 4
FNET_GROUP_W = 64
D_FNET = FNET_GROUPS * FNET_GROUP_W
D_SHIFTED = 3 * D_RWKV + 2 * DECAY_LORA + 2 * AAA_LORA + GATE_LORA
D_AB_IN = D_SHIFTED + D_FNET
HY_BANDS = 16
HY_EMB = 2 * HY_BANDS + 1
HY_ORDER = 64
PEER_HEADS = 8
N_KEYS = 128
N_EXPERTS = N_KEYS * N_KEYS
PEER_DK = 256
PEER_TOPK = 16
PEER_BLOCK = 128
LN_EPS = 1e-5
DN_ALPHA = (2 * DEPTH) ** 0.25
DN_BETA = (8 * DEPTH) ** -0.25
F32 = jnp.float32


def layer_norm(x, g=None, b=None):
    xf = x.astype(F32)
    mu = jnp.mean(xf, axis=-1, keepdims=True)
    var = jnp.mean(jnp.square(xf - mu), axis=-1, keepdims=True)
    y = (xf - mu) * lax.rsqrt(var + LN_EPS)
    if g is not None:
        y = y * g.astype(F32) + b.astype(F32)
    return y.astype(x.dtype)


def shift_prev(z):
    return jnp.pad(z, ((0, 0), (1, 0), (0, 0)))[:, :-1]


def shift_next(z):
    return jnp.pad(z, ((0, 0), (0, 1), (0, 0)))[:, 1:]


def modulation(cvec, w_mod, b_mod):
    m = jax.nn.silu(cvec) @ w_mod + b_mod
    return jnp.split(m, 6, axis=-1)


def grid_pos_embed(n_tokens):
    rows = n_tokens // GRID_W
    row = jnp.repeat(jnp.arange(rows, dtype=F32), GRID_W)
    col = jnp.tile(jnp.arange(GRID_W, dtype=F32), rows)
    quarter = D_MODEL // 4
    omega = 1.0 / (10000.0 ** (jnp.arange(quarter, dtype=F32) / quarter))
    ra = row[:, None] * omega[None, :]
    ca = col[:, None] * omega[None, :]
    return jnp.concatenate([jnp.sin(ra), jnp.cos(ra), jnp.sin(ca), jnp.cos(ca)], axis=-1)


BF16 = jnp.bfloat16
VMEM_LIMIT_BYTES = 56 * 1024 * 1024
LANES = 128
WKV_CHUNK = 16
WKV_UNROLL = 4


def _split(x):
    hi = x.astype(BF16)
    return hi, (x - hi.astype(F32)).astype(BF16)


def _dot3(a, b, dims=(((1,), (0,)), ((), ()))):
    ah, al = _split(a)
    bh, bl = _split(b)
    mm = functools.partial(lax.dot_general, dimension_numbers=dims, preferred_element_type=F32)
    return mm(ah, bh) + (mm(al, bh) + mm(ah, bl))


_NT = (((1,), (1,)), ((), ()))


def _pair_rows(z, lo):
    return jnp.concatenate([jnp.where(lo, z, 0.0), jnp.where(lo, 0.0, z)], axis=0)


def _chunk_transition(lw, ak, kka, k, r, v, rev):
    C = WKV_CHUNK
    t_i = lax.broadcasted_iota(jnp.int32, (C, C), 0)
    s_i = lax.broadcasted_iota(jnp.int32, (C, C), 1)
    upto = (s_i >= t_i) if rev else (s_i <= t_i)
    cum = _dot3(jnp.where(upto, 1.0, 0.0), lw)
    yield
    tot = cum[0:1] if rev else cum[C - 1:C]
    a_t = ak * jnp.exp(cum - lw)
    r_t = r * jnp.exp(cum)
    inv = jnp.exp(-cum)
    b_t, k_t = kka * inv, k * inv
    end = jnp.exp(tot - cum)
    b_e, k_e = kka * end, k * end
    lo = lax.broadcasted_iota(jnp.int32, (1, LANES), 1) < HEAD_SIZE
    lo2 = jnp.concatenate([lo, lo], axis=1)
    t2 = lax.broadcasted_iota(jnp.int32, (C, 2 * C), 0)
    s2 = lax.broadcasted_iota(jnp.int32, (C, 2 * C), 1) % C
    before2 = (s2 > t2) if rev else (s2 < t2)
    upto2 = (s2 >= t2) if rev else (s2 <= t2)
    x = jnp.concatenate([a_t, r_t], axis=0)
    gb = _dot3(x, _pair_rows(b_t, lo), _NT)
    gk = _dot3(x, _pair_rows(k_t, lo), _NT)
    nt = _dot3(b_t, _pair_rows(a_t, lo), _NT)
    yield
    lak = jnp.where(before2, gk[0:C], 0.0)
    lrb = jnp.where(upto2, gb[C:2 * C], 0.0)
    lrk = jnp.where(upto2, gk[C:2 * C], 0.0)
    nt = jnp.where((t2 > s2) if rev else (t2 < s2), nt, 0.0)
    lane2 = lax.broadcasted_iota(jnp.int32, (C, 2 * C), 1)
    row2 = t2
    lakv = _dot3(lak, _pair_rows(v, lo))
    y1 = _dot3(lrk, _pair_rows(v, lo))
    tinv = jnp.zeros((C, 2 * C), F32)
    for t in (range(C - 1, -1, -1) if rev else range(C)):
        coef = jnp.take_along_axis(nt, t + C * (lane2 // C), axis=1)
        new = jnp.where(lane2 % C == t, 1.0, 0.0)[0:1] + jnp.sum(coef * tinv, axis=0, keepdims=True)
        tinv = jnp.where(row2 == t, new, tinv)
        yield
    m1w2 = _dot3(tinv, _pair_rows(jnp.concatenate([a_t, lakv], axis=1), lo2))
    yield
    my = _dot3(lrb, _pair_rows(m1w2, lo2))
    z = jnp.concatenate([b_e, k_e], axis=0)
    rhs = jnp.concatenate([m1w2, jnp.concatenate([jnp.zeros_like(v), v], axis=1)], axis=0)
    p = _dot3(z.T, rhs)
    yield
    m2 = r_t + my[:, :LANES]
    y1 = my[:, LANES:] + y1
    r_l = lax.broadcasted_iota(jnp.int32, (LANES, LANES), 0)
    c_l = lax.broadcasted_iota(jnp.int32, (LANES, LANES), 1)
    same = (r_l < HEAD_SIZE) == (c_l < HEAD_SIZE)
    m3 = jnp.where(same, p[:, :LANES], 0.0) + jnp.where(r_l == c_l, jnp.exp(tot), 0.0)
    s1 = jnp.where(same, p[:, LANES:], 0.0)
    yield m2, y1, m3, s1


def _lockstep(gens):
    out = [None] * len(gens)
    live = list(range(len(gens)))
    while live:
        for i in list(live):
            try:
                val = next(gens[i])
            except StopIteration:
                live.remove(i)
            else:
                out[i] = val if val is not None else out[i]
    return out


def _wkv_kernel(ak_ref, r_ref, v_ref, lwf_ref, kkaf_ref, kf_ref, lwb_ref, kkab_ref, kb_ref, s0f_ref, s0b_ref,
                y_ref, sf_ref, sb_ref, yb_ref):
    L = v_ref.shape[1]
    C = WKV_CHUNK
    n = L // C

    def load(ref, o):
        return ref[0, pl.ds(o, C), :]

    def body(c, carry):
        sf, sb = carry
        offs_f = [pl.multiple_of((c * WKV_UNROLL + i) * C, C) for i in range(WKV_UNROLL)]
        offs_b = [pl.multiple_of((n - 1 - c * WKV_UNROLL - i) * C, C) for i in range(WKV_UNROLL)]
        gf = [_chunk_transition(load(lwf_ref, o), load(ak_ref, o), load(kkaf_ref, o), load(kf_ref, o),
                                load(r_ref, o), load(v_ref, o), False) for o in offs_f]
        gb = [_chunk_transition(load(lwb_ref, o), load(ak_ref, o), load(kkab_ref, o), load(kb_ref, o),
                                load(r_ref, o), load(v_ref, o), True) for o in offs_b]
        done = _lockstep(gf + gb)
        tf, tb = done[:WKV_UNROLL], done[WKV_UNROLL:]
        for i in range(WKV_UNROLL):
            m2, y1, m3, s1 = tf[i]
            y_ref[0, pl.ds(offs_f[i], C), :] = _dot3(m2, sf) + y1
            sf = _dot3(m3, sf) + s1
            m2, y1, m3, s1 = tb[i]
            yb_ref[pl.ds(offs_b[i], C), :] = _dot3(m2, sb) + y1
            sb = _dot3(m3, sb) + s1
        return sf, sb

    sf, sb = lax.fori_loop(0, n // WKV_UNROLL, body, (s0f_ref[0, 0], s0b_ref[0, 0]))
    sf_ref[0, 0] = sf
    sb_ref[0, 0] = sb
    y_ref[0] = y_ref[0] + yb_ref[...]


def wkv_pallas(ak, r, v, lwf, kkaf, kf, lwb, kkab, kb, s0f, s0b):
    B, L, Cd = v.shape
    npair = Cd // LANES
    seq = pl.BlockSpec((1, L, LANES), lambda b, p: (b, 0, p))
    st = pl.BlockSpec((1, 1, LANES, LANES), lambda b, p: (b, p, 0, 0))
    st_shape = jax.ShapeDtypeStruct((B, npair, LANES, LANES), F32)
    return pl.pallas_call(
        _wkv_kernel, grid=(B, npair),
        in_specs=[seq] * 9 + [st, st],
        out_specs=[seq, st, st],
        out_shape=[jax.ShapeDtypeStruct((B, L, Cd), F32), st_shape, st_shape],
        scratch_shapes=[pltpu.VMEM((L, LANES), F32)],
        compiler_params=pltpu.CompilerParams(dimension_semantics=("arbitrary", "arbitrary"),
                                             vmem_limit_bytes=VMEM_LIMIT_BYTES),
        name="wkv_scan",
    )(ak, r, v, lwf, kkaf, kf, lwb, kkab, kb, s0f, s0b)


def _pack_state(s):
    B = s.shape[0]
    st = jnp.swapaxes(s, -1, -2).reshape(B, RWKV_HEADS // 2, 2, HEAD_SIZE, HEAD_SIZE)
    z = jnp.zeros_like(st[:, :, 0])
    top = jnp.concatenate([st[:, :, 0], z], axis=-1)
    bot = jnp.concatenate([z, st[:, :, 1]], axis=-1)
    return jnp.concatenate([top, bot], axis=-2)


def _unpack_state(sd):
    B = sd.shape[0]
    a = sd[:, :, :HEAD_SIZE, :HEAD_SIZE]
    b = sd[:, :, HEAD_SIZE:, HEAD_SIZE:]
    st = jnp.stack([a, b], axis=2).reshape(B, RWKV_HEADS, HEAD_SIZE, HEAD_SIZE)
    return jnp.swapaxes(st, -1, -2)


def rwkv_mix(zs, e, P, s0_f, s0_b):
    B, L, _ = zs.shape
    splits = [D_RWKV, 2 * D_RWKV, 3 * D_RWKV,
              3 * D_RWKV + DECAY_LORA, 3 * D_RWKV + 2 * DECAY_LORA,
              3 * D_RWKV + 2 * DECAY_LORA + AAA_LORA, 3 * D_RWKV + 2 * DECAY_LORA + 2 * AAA_LORA]
    r, k, v, zw_f, zw_b, za_f, za_b, zg = jnp.split(zs, splits, axis=-1)

    def heads(t):
        return t.astype(F32).reshape(B, L, RWKV_HEADS, HEAD_SIZE)

    r_h, k_h, v_h = heads(r), heads(k), heads(v)
    kk = heads(k * P['rwkv_k_k'][e])
    kk = kk * lax.rsqrt(jnp.sum(kk * kk, axis=-1, keepdims=True) + 1e-12)
    k_a = P['rwkv_k_a'][e].astype(F32).reshape(RWKV_HEADS, HEAD_SIZE)
    k_dirs, ops = [], []
    flat = lambda t: t.reshape(B, L, D_RWKV)
    for d, (zw, za) in enumerate(((zw_f, za_f), (zw_b, za_b))):
        w_raw = (P['rwkv_w0'][e, d] + jnp.tanh(zw) @ P['rwkv_w2'][e, d]).astype(F32)
        log_decay = -jnp.exp(-jax.nn.softplus(-w_raw) - 0.5)
        a = heads(jax.nn.sigmoid((P['rwkv_a0'][e, d] + za @ P['rwkv_a2'][e, d]).astype(F32)))
        k_d = k_h * (1.0 + (a - 1.0) * k_a)
        ops += [log_decay, flat(kk * a), flat(k_d)]
        k_dirs.append(k_d)
    y, sd_f, sd_b = wkv_pallas(flat(-kk), flat(r_h), flat(v_h), *ops,
                               _pack_state(s0_f.astype(F32)), _pack_state(s0_b.astype(F32)))
    finals = [_unpack_state(sd_f), _unpack_state(sd_b)]
    y = y.reshape(B, L, RWKV_HEADS, HEAD_SIZE)
    mu = jnp.mean(y, axis=-1, keepdims=True)
    var = jnp.mean(jnp.square(y - mu), axis=-1, keepdims=True)
    y = ((y - mu) * lax.rsqrt(var + GN_EPS)).reshape(B, L, D_RWKV)
    y = y * P['rwkv_gn_g'][e].astype(F32) + P['rwkv_gn_b'][e].astype(F32)
    k_bonus = 0.5 * (k_dirs[0] + k_dirs[1])
    bonus = jnp.sum(r_h * k_bonus * P['rwkv_r_k'][e].astype(F32), axis=-1, keepdims=True) * v_h
    y = y + bonus.reshape(B, L, D_RWKV)
    g = jax.nn.sigmoid(zg) @ P['rwkv_g2'][e]
    return y.astype(zs.dtype) * g, finals[0], finals[1]


def _dot3_pre(ah, al, b):
    bh, bl = _split(b)
    mm = functools.partial(jnp.dot, preferred_element_type=F32)
    return mm(ah, bh) + (mm(al, bh) + mm(ah, bl))


def _trig_tables(phase, den):
    ang = (phase % den).astype(F32) * (2.0 * math.pi / den)
    return _split(jnp.cos(ang)) + _split(jnp.sin(ang))


def _grid2(n, m):
    return lax.broadcasted_iota(jnp.int32, (n, m), 0), lax.broadcasted_iota(jnp.int32, (n, m), 1)


FOURIER_TL = 512


def _fourier_kernel(z_ref, cw_ref, sw_ref, ch_ref, cl_ref, sh_ref, sl_ref, o_ref, zc_ref, zs_ref):
    @pl.when(pl.program_id(1) == 0)
    def _():
        z = z_ref[0]
        for ref, w_ref in ((zc_ref, cw_ref), (zs_ref, sw_ref)):
            hi, lo = _split(_dot3(z, w_ref[...]))
            ref[0] = hi
            ref[1] = lo

    mm = functools.partial(jnp.dot, preferred_element_type=F32)

    def prod(th, tl, ref):
        return mm(th[...], ref[0]) + (mm(tl[...], ref[0]) + mm(th[...], ref[1]))

    o_ref[0] = prod(ch_ref, cl_ref, zc_ref) - prod(sh_ref, sl_ref, zs_ref)


def fourier_mix(zf):
    B, L, W = zf.shape
    j, c = _grid2(W, W)
    ang = ((j * c) % FNET_GROUP_W).astype(F32) * (2.0 * math.pi / FNET_GROUP_W)
    same = (j // FNET_GROUP_W == c // FNET_GROUP_W).astype(F32) * (L * FNET_GROUP_W) ** -0.5
    k, t = _grid2(L, L)
    tables = _trig_tables(k * t, L)
    tl = min(FOURIER_TL, L)
    tab = pl.BlockSpec((tl, L), lambda b, i: (i, 0))
    small = pl.BlockSpec((W, W), lambda b, i: (0, 0))
    return pl.pallas_call(
        _fourier_kernel, grid=(B, L // tl),
        in_specs=[pl.BlockSpec((1, L, W), lambda b, i: (b, 0, 0)), small, small, tab, tab, tab, tab],
        out_specs=pl.BlockSpec((1, tl, W), lambda b, i: (b, i, 0)),
        out_shape=jax.ShapeDtypeStruct((B, L, W), F32),
        scratch_shapes=[pltpu.VMEM((2, L, W), BF16), pltpu.VMEM((2, L, W), BF16)],
        compiler_params=pltpu.CompilerParams(dimension_semantics=("arbitrary", "arbitrary"),
                                             vmem_limit_bytes=VMEM_LIMIT_BYTES),
        name="fourier_mix",
    )(zf, jnp.cos(ang) * same, jnp.sin(ang) * same, *tables)


def ab_mixer(z, e, P, s0_f, s0_b):
    zs, zf = z[..., :D_SHIFTED], z[..., D_SHIFTED:]
    zs = zs + P['ab_mu'][e] * (0.5 * (shift_prev(zs) + shift_next(zs)) - zs)
    y_a, s_f, s_b = rwkv_mix(zs, e, P, s0_f, s0_b)
    y_b = fourier_mix(zf)
    return y_a, y_b, s_f, s_b


def hyena_filters(L, w1, b1, w2, b2, w3, freq, decay):
    pos = jnp.arange(L, dtype=F32)[:, None]
    t = pos / (L - 1)
    bands = jnp.linspace(1e-4, HY_BANDS - 1, HY_BANDS, dtype=F32)[None, :]
    ang = 2.0 * math.pi * pos / L * bands
    feats = jnp.concatenate([t, jnp.cos(ang), -jnp.sin(ang)], axis=-1)
    fr = freq.astype(F32)
    hid = jnp.sin(fr * (feats @ w1.astype(F32) + b1.astype(F32)))
    hid = jnp.sin(fr * (hid @ w2.astype(F32) + b2.astype(F32)))
    filt = (hid @ w3.astype(F32)) * jnp.exp(-t * jnp.abs(decay.astype(F32)))
    return filt[:, :D_MODEL], filt[:, D_MODEL:]


CONV_KT = 256
CONV_DC = 512


def _filt_kernel(ch_ref, cl_ref, sh_ref, sl_ref, hs_ref, hd_ref, hr_ref, hi_ref):
    hr_ref[...] = _dot3_pre(ch_ref[...], cl_ref[...], hs_ref[...])
    hi_ref[...] = _dot3_pre(sh_ref[...], sl_ref[...], hd_ref[...])


def _conv_kernel(u_ref, skip_ref, ch_ref, cl_ref, sh_ref, sl_ref, cth_ref, ctl_ref, sth_ref, stl_ref,
                 hr_ref, hi_ref, o_ref, acc_ref, us_ref):
    k = pl.program_id(2)

    @pl.when(k == 0)
    def _():
        hi, lo = _split(u_ref[0])
        us_ref[0] = hi
        us_ref[1] = lo
        acc_ref[...] = jnp.zeros_like(acc_ref)

    mm = functools.partial(jnp.dot, preferred_element_type=F32)

    def fwd(th, tl):
        return mm(th[...], us_ref[0]) + (mm(tl[...], us_ref[0]) + mm(th[...], us_ref[1]))

    a, b = fwd(ch_ref, cl_ref), fwd(sh_ref, sl_ref)
    hr, hi = hr_ref[...], hi_ref[...]
    acc_ref[...] += (_dot3_pre(cth_ref[...], ctl_ref[...], a * hr + b * hi)
                     - _dot3_pre(sth_ref[...], stl_ref[...], a * hi - b * hr))

    @pl.when(k == pl.num_programs(2) - 1)
    def _():
        o_ref[0] = acc_ref[...] * (1.0 / u_ref.shape[1]) + u_ref[0] * skip_ref[...]


def long_conv(u, h_f, h_b, skip):
    B, L, D = u.shape
    kt, dc = min(CONV_KT, L), CONV_DC
    k, s = _grid2(L, L)
    phase = (2 * k + 1) * s
    fwd_t = _trig_tables(phase, 4 * L)
    inv_t = _trig_tables(phase.T, 4 * L)
    h_b0 = h_b.at[0].set(0.0)
    row_t = pl.BlockSpec((kt, L), lambda i, j: (i, 0))
    filt = pl.BlockSpec((L, dc), lambda i, j: (0, j))
    spec = pl.BlockSpec((kt, dc), lambda i, j: (i, j))
    hr, hi = pl.pallas_call(
        _filt_kernel, grid=(L // kt, D // dc),
        in_specs=[row_t] * 4 + [filt, filt], out_specs=[spec, spec],
        out_shape=[jax.ShapeDtypeStruct((L, D), F32)] * 2,
        compiler_params=pltpu.CompilerParams(dimension_semantics=("arbitrary", "arbitrary"),
                                             vmem_limit_bytes=VMEM_LIMIT_BYTES),
        name="hyena_filter_spectrum",
    )(*fwd_t, h_f + h_b0, h_b0 - h_f)
    seq = pl.BlockSpec((1, L, dc), lambda b, j, i: (b, 0, j))
    row_t = pl.BlockSpec((kt, L), lambda b, j, i: (i, 0))
    col_t = pl.BlockSpec((L, kt), lambda b, j, i: (0, i))
    spec = pl.BlockSpec((kt, dc), lambda b, j, i: (i, j))
    return pl.pallas_call(
        _conv_kernel, grid=(B, D // dc, L // kt),
        in_specs=[seq, pl.BlockSpec((1, dc), lambda b, j, i: (0, j))] + [row_t] * 4 + [col_t] * 4 + [spec, spec],
        out_specs=seq,
        out_shape=jax.ShapeDtypeStruct((B, L, D), F32),
        scratch_shapes=[pltpu.VMEM((L, dc), F32), pltpu.VMEM((2, L, dc), BF16)],
        compiler_params=pltpu.CompilerParams(dimension_semantics=("arbitrary", "arbitrary", "arbitrary"),
                                             vmem_limit_bytes=VMEM_LIMIT_BYTES),
        name="hyena_long_conv",
    )(u, skip.reshape(1, D), *fwd_t, *inv_t, hr, hi)


def hyena_mixer(z, o, P):
    L = z.shape[1]
    cw = P['hy_conv_w'][o]
    z = cw[0] * shift_prev(z) + cw[1] * z + cw[2] * shift_next(z) + P['hy_conv_b'][o]
    x0, x1, v = jnp.split(z, 3, axis=-1)
    h_f, h_b = hyena_filters(L, P['hy_filt_w1'][o], P['hy_filt_b1'][o], P['hy_filt_w2'][o],
                             P['hy_filt_b2'][o], P['hy_filt_w3'][o], P['hy_filt_freq'][o],
                             P['hy_filt_decay'][o])
    y = long_conv((v * x1).astype(F32), h_f, h_b, P['hy_skip'][o].astype(F32))
    return y * x0


NEG_INF = float('-inf')
ROUTE_TT = 256
EXPERT_TT = 512
EXPERT_ET = 512


def _top_rows(x, n):
    row = lax.broadcasted_iota(jnp.int32, (n, x.shape[1]), 0)
    out = jnp.zeros((n, x.shape[1]), F32)
    for j in range(n):
        m = jnp.max(x, axis=0, keepdims=True)
        out = jnp.where(row == j, m, out)
        x = jnp.where(x == m, NEG_INF, x)
    return out


def _route_kernel(xT_ref, wqT_ref, k1_ref, k2_ref, s1_ref, s2_ref, p1_ref, p2_ref, tau_ref, qT_ref):
    qT_ref[...] = jnp.dot(wqT_ref[...], xT_ref[...], preferred_element_type=F32)
    half = PEER_DK // 2
    lanes = 128

    def head(h, carry):
        base = pl.multiple_of(h * PEER_DK, PEER_DK)
        q1 = qT_ref[pl.ds(base, half), :].astype(BF16)
        q2 = qT_ref[pl.ds(base + half, half), :].astype(BF16)
        s1 = jnp.dot(k1_ref[h], q1, preferred_element_type=F32)
        s2 = jnp.dot(k2_ref[h], q2, preferred_element_type=F32)
        s1_ref[h] = s1
        s2_ref[h] = s2
        for c in range(s1.shape[1] // lanes):
            sl = slice(c * lanes, (c + 1) * lanes)
            s1c, s2c = s1[:, sl], s2[:, sl]
            a = _top_rows(s1c, PEER_TOPK)
            b = _top_rows(s2c, PEER_TOPK)
            parts = [a[0:1] + b]
            for i in range(1, 8):
                parts.append(a[i:i + 1] + b[0:8])
            parts.append(a[8:16] + b[0:1])
            cand = jnp.concatenate(parts, axis=0)
            top = cand[0:1]
            work = cand
            tau = jnp.zeros_like(top)
            left = jnp.full(top.shape, float(PEER_TOPK), F32)
            for _ in range(PEER_TOPK):
                m = jnp.max(work, axis=0, keepdims=True)
                hit = work == m
                tau = jnp.where(left > 0.0, m, tau)
                left = left - jnp.sum(jnp.where(hit, 1.0, 0.0), axis=0, keepdims=True)
                work = jnp.where(hit, NEG_INF, work)
            z = jnp.sum(jnp.where(cand >= tau, jnp.exp(cand - top), 0.0), axis=0, keepdims=True)
            tau_ref[h, :, sl] = tau
            p1_ref[h, :, sl] = jnp.exp(s1c - a[0:1]) / z
            p2_ref[h, :, sl] = jnp.exp(s2c - b[0:1])
        return carry

    lax.fori_loop(0, PEER_HEADS, head, 0)


def _expert_kernel(xT_ref, s1_ref, s2_ref, p1_ref, p2_ref, tau_ref, u_ref, vT_ref, o_ref, w_ref):
    i = pl.program_id(1)
    last = pl.num_programs(1) - 2

    @pl.when(i == 0)
    def _():
        o_ref[...] = jnp.zeros_like(o_ref)
        w_ref[1] = jnp.zeros(w_ref.shape[1:], BF16)

    slot = i % 2
    n1 = u_ref.shape[0] // N_KEYS
    o_ref[...] += jnp.dot(vT_ref[...], w_ref[1 - slot], preferred_element_type=F32)
    pre = jnp.dot(u_ref[...], xT_ref[...], preferred_element_type=F32)
    for g in range(n1):
        rows = slice(g * N_KEYS, (g + 1) * N_KEYS)
        i1 = jnp.minimum(i, last) * n1 + g
        gate = jnp.zeros((N_KEYS, xT_ref.shape[1]), F32)
        for h in range(PEER_HEADS):
            s = s1_ref[h, pl.ds(i1, 1), :] + s2_ref[h]
            p = p1_ref[h, pl.ds(i1, 1), :] * p2_ref[h]
            gate = gate + jnp.where(s >= tau_ref[h], p, 0.0)
        w_ref[slot, rows, :] = (jax.nn.gelu(pre[rows]) * gate).astype(BF16)


def peer_pallas(hT, wqT, k1, k2, u, vT):
    D, T = hT.shape
    H, NK = PEER_HEADS, N_KEYS
    route_shapes = [jax.ShapeDtypeStruct((H, NK, T), F32)] * 4 + [jax.ShapeDtypeStruct((H, 1, T), F32)]
    tt = ROUTE_TT
    s1, s2, p1, p2, tau = pl.pallas_call(
        _route_kernel,
        grid=(T // tt,),
        in_specs=[pl.BlockSpec((D, tt), lambda j: (0, j)),
                  pl.BlockSpec((H * PEER_DK, D), lambda j: (0, 0)),
                  pl.BlockSpec((H, NK, PEER_DK // 2), lambda j: (0, 0, 0)),
                  pl.BlockSpec((H, NK, PEER_DK // 2), lambda j: (0, 0, 0))],
        out_specs=[pl.BlockSpec((H, NK, tt), lambda j: (0, 0, j))] * 4 + [pl.BlockSpec((H, 1, tt), lambda j: (0, 0, j))],
        out_shape=route_shapes,
        scratch_shapes=[pltpu.VMEM((H * PEER_DK, tt), F32)],
        compiler_params=pltpu.CompilerParams(dimension_semantics=("arbitrary",),
                                             vmem_limit_bytes=VMEM_LIMIT_BYTES),
        name="peer_route",
    )(hT, wqT, k1, k2)
    tt, et = EXPERT_TT, EXPERT_ET
    route_spec = pl.BlockSpec((H, NK, tt), lambda j, i: (0, 0, j))
    n_et = N_EXPERTS // et
    return pl.pallas_call(
        _expert_kernel,
        grid=(T // tt, n_et + 1),
        in_specs=[pl.BlockSpec((D, tt), lambda j, i: (0, j)),
                  route_spec, route_spec, route_spec, route_spec,
                  pl.BlockSpec((H, 1, tt), lambda j, i: (0, 0, j)),
                  pl.BlockSpec((et, D), lambda j, i: (jnp.minimum(i, n_et - 1), 0)),
                  pl.BlockSpec((D, et), lambda j, i: (0, jnp.maximum(i - 1, 0)))],
        out_specs=pl.BlockSpec((D, tt), lambda j, i: (0, j)),
        out_shape=jax.ShapeDtypeStruct((D, T), F32),
        scratch_shapes=[pltpu.VMEM((2, et, tt), BF16)],
        compiler_params=pltpu.CompilerParams(dimension_semantics=("arbitrary", "arbitrary"),
                                             vmem_limit_bytes=VMEM_LIMIT_BYTES),
        name="peer_experts",
    )(hT, s1, s2, p1, p2, tau, u, vT)


TOKEN_TM = 512
MOD_ROWS = 6


def _ln_rows(x):
    mu = jnp.mean(x, axis=-1, keepdims=True)
    xc = x - mu
    return xc * lax.rsqrt(jnp.mean(xc * xc, axis=-1, keepdims=True) + LN_EPS)


def _mod_kernel(c_ref, w_ref, b_ref, o_ref):
    act = jax.nn.silu(c_ref[...]).astype(BF16)
    o_ref[...] = jnp.dot(act, w_ref[...].astype(BF16), preferred_element_type=F32) + b_ref[...]


def modulation_pallas(cvecs, w_mod, b_mod):
    R, D = cvecs.shape
    N = w_mod.shape[1]
    out = pl.pallas_call(
        _mod_kernel, grid=(N // D,),
        in_specs=[pl.BlockSpec((R, D), lambda n: (0, 0)),
                  pl.BlockSpec((D, D), lambda n: (0, n)),
                  pl.BlockSpec((1, D), lambda n: (0, n))],
        out_specs=pl.BlockSpec((R, D), lambda n: (0, n)),
        out_shape=jax.ShapeDtypeStruct((R, N), F32),
        name="modulation",
    )(cvecs, w_mod, b_mod.reshape(1, N))
    return out.reshape(R, MOD_ROWS, D)


def _mod_spec(tm, rows_per_mod, mod_off):
    return pl.BlockSpec((1, MOD_ROWS, D_MODEL), lambda i: (mod_off + (i * tm) // rows_per_mod, 0, 0))


def _token_params():
    return pltpu.CompilerParams(dimension_semantics=("arbitrary",), vmem_limit_bytes=VMEM_LIMIT_BYTES)


def _lnmm_kernel(x_ref, mod_ref, w_ref, o_ref):
    h = _ln_rows(x_ref[...]) * (1.0 + mod_ref[0, 1:2, :]) + mod_ref[0, 0:1, :]
    o_ref[...] = jnp.dot(h.astype(BF16), w_ref[...], preferred_element_type=F32)


def ln_mod_matmul(x, mods, rows_per_mod, mod_off, w):
    T, D = x.shape
    N = w.shape[1]
    tm = TOKEN_TM
    return pl.pallas_call(
        _lnmm_kernel, grid=(T // tm,),
        in_specs=[pl.BlockSpec((tm, D), lambda i: (i, 0)),
                  _mod_spec(tm, rows_per_mod, mod_off),
                  pl.BlockSpec((D, N), lambda i: (0, 0))],
        out_specs=pl.BlockSpec((tm, N), lambda i: (i, 0)),
        out_shape=jax.ShapeDtypeStruct((T, N), F32),
        compiler_params=_token_params(),
        name="ln_mod_matmul",
    )(x, mods, w)


def _proj_ln_kernel(*refs, n_in):
    a_refs, w_refs = refs[:n_in], refs[n_in:2 * n_in]
    x_ref, mod_ref, ln_ref, xo_ref, hT_ref = refs[2 * n_in:]
    y = jnp.dot(a_refs[0][...].astype(BF16), w_refs[0][...], preferred_element_type=F32)
    for a_ref, w_ref in zip(a_refs[1:], w_refs[1:]):
        y = y + jnp.dot(a_ref[...].astype(BF16), w_ref[...], preferred_element_type=F32)
    x = _ln_rows(DN_ALPHA * x_ref[...] + mod_ref[0, 2:3, :] * y) * ln_ref[0:1, :] + ln_ref[1:2, :]
    xo_ref[...] = x
    h = _ln_rows(x) * (1.0 + mod_ref[0, 4:5, :]) + mod_ref[0, 3:4, :]
    hT_ref[...] = h.T.astype(BF16)


def proj_ln(acts, ws, x, mods, rows_per_mod, mod_off, ln):
    T, D = x.shape
    tm = TOKEN_TM
    return pl.pallas_call(
        functools.partial(_proj_ln_kernel, n_in=len(acts)), grid=(T // tm,),
        in_specs=[pl.BlockSpec((tm, a.shape[1]), lambda i: (i, 0)) for a in acts]
        + [pl.BlockSpec(w.shape, lambda i: (0, 0)) for w in ws]
        + [pl.BlockSpec((tm, D), lambda i: (i, 0)), _mod_spec(tm, rows_per_mod, mod_off),
           pl.BlockSpec((2, D), lambda i: (0, 0))],
        out_specs=[pl.BlockSpec((tm, D), lambda i: (i, 0)), pl.BlockSpec((D, tm), lambda i: (0, i))],
        out_shape=[jax.ShapeDtypeStruct((T, D), F32), jax.ShapeDtypeStruct((D, T), BF16)],
        compiler_params=_token_params(),
        name="proj_ln",
    )(*acts, *ws, x, mods, ln)


def _peer_out_kernel(yT_ref, x_ref, mod_ref, ln_ref, xo_ref):
    y = yT_ref[...].T
    xo_ref[...] = _ln_rows(DN_ALPHA * x_ref[...] + mod_ref[0, 5:6, :] * y) * ln_ref[0:1, :] + ln_ref[1:2, :]


def peer_out_ln(yT, tile_off, x, mods, rows_per_mod, mod_off, ln):
    T, D = x.shape
    tm = TOKEN_TM
    return pl.pallas_call(
        _peer_out_kernel, grid=(T // tm,),
        in_specs=[pl.BlockSpec((D, tm), lambda i: (0, tile_off + i)),
                  pl.BlockSpec((tm, D), lambda i: (i, 0)), _mod_spec(tm, rows_per_mod, mod_off),
                  pl.BlockSpec((2, D), lambda i: (0, 0))],
        out_specs=pl.BlockSpec((tm, D), lambda i: (i, 0)),
        out_shape=jax.ShapeDtypeStruct((T, D), F32),
        compiler_params=_token_params(),
        name="peer_out_ln",
    )(yT, x, mods, ln)


def kernel(x_prompt, x_sample, c, state_rwkv_fwd, state_rwkv_bwd, c_ctx, w_mod, b_mod, ln_mix_g, ln_mix_b, ln_ffn_g, ln_ffn_b, peer_wq, peer_k1, peer_k2, peer_u, peer_v, ab_w_in, ab_mu, rwkv_w0, rwkv_w2, rwkv_a0, rwkv_a2, rwkv_g2, rwkv_k_k, rwkv_k_a, rwkv_r_k, rwkv_gn_g, rwkv_gn_b, ab_w_out, hy_w_in, hy_conv_w, hy_conv_b, hy_filt_w1, hy_filt_b1, hy_filt_w2, hy_filt_b2, hy_filt_w3, hy_filt_freq, hy_filt_decay, hy_skip, hy_w_out):
    P = {
        'w_mod': w_mod, 'b_mod': b_mod,
        'ln_mix_g': ln_mix_g, 'ln_mix_b': ln_mix_b, 'ln_ffn_g': ln_ffn_g, 'ln_ffn_b': ln_ffn_b,
        'peer_wq': peer_wq, 'peer_k1': peer_k1, 'peer_k2': peer_k2, 'peer_u': peer_u, 'peer_v': peer_v,
        'ab_w_in': ab_w_in, 'ab_mu': ab_mu, 'rwkv_w0': rwkv_w0, 'rwkv_w2': rwkv_w2,
        'rwkv_a0': rwkv_a0, 'rwkv_a2': rwkv_a2, 'rwkv_g2': rwkv_g2, 'rwkv_k_k': rwkv_k_k,
        'rwkv_k_a': rwkv_k_a, 'rwkv_r_k': rwkv_r_k, 'rwkv_gn_g': rwkv_gn_g, 'rwkv_gn_b': rwkv_gn_b,
        'ab_w_out': ab_w_out,
        'hy_w_in': hy_w_in, 'hy_conv_w': hy_conv_w, 'hy_conv_b': hy_conv_b,
        'hy_filt_w1': hy_filt_w1, 'hy_filt_b1': hy_filt_b1, 'hy_filt_w2': hy_filt_w2,
        'hy_filt_b2': hy_filt_b2, 'hy_filt_w3': hy_filt_w3, 'hy_filt_freq': hy_filt_freq,
        'hy_filt_decay': hy_filt_decay, 'hy_skip': hy_skip, 'hy_w_out': hy_w_out,
    }
    D = D_MODEL
    zero_state = jnp.zeros((BATCH, RWKV_HEADS, HEAD_SIZE, HEAD_SIZE), F32)
    x_lat = x_sample + grid_pos_embed(DEC_SEQ).astype(x_sample.dtype)
    cvecs = jnp.concatenate([c_ctx[None, :], c], axis=0)
    groups = [dict(x=x_prompt.reshape(-1, D), B=BATCH, L=SEQ, per_mod=BATCH * SEQ, mod_off=0),
              dict(x=x_lat.reshape(-1, D), B=DEC_BATCH, L=DEC_SEQ, per_mod=DEC_SEQ, mod_off=1)]
    new_f, new_b = [], []
    for layer in range(DEPTH):
        mods = modulation_pallas(cvecs, w_mod[layer], b_mod[layer])
        ln_mix = jnp.stack([ln_mix_g[layer], ln_mix_b[layer]])
        ln_ffn = jnp.stack([ln_ffn_g[layer], ln_ffn_b[layer]])
        e = layer // 2
        hts = []
        for gi, g in enumerate(groups):
            mod_args = (mods, g['per_mod'], g['mod_off'])
            if layer % 2 == 0:
                z = ln_mod_matmul(g['x'], *mod_args, ab_w_in[e].astype(BF16)).reshape(g['B'], g['L'], D_AB_IN)
                s0 = (zero_state, zero_state) if gi == 0 else (state_rwkv_fwd[:, e], state_rwkv_bwd[:, e])
                y_a, y_b, s_f, s_b = ab_mixer(z, e, P, *s0)
                if gi == 0:
                    new_f.append(s_f)
                    new_b.append(s_b)
                w_out = ab_w_out[e].astype(BF16)
                acts = [y_a.reshape(-1, D_RWKV), y_b.reshape(-1, D_FNET)]
                ws = [w_out[:D_RWKV], w_out[D_RWKV:]]
            else:
                z = ln_mod_matmul(g['x'], *mod_args, hy_w_in[e].astype(BF16)).reshape(g['B'], g['L'], 3 * D)
                acts = [hyena_mixer(z, e, P).reshape(-1, D)]
                ws = [hy_w_out[e].astype(BF16)]
            g['x'], ht = proj_ln(acts, ws, g['x'], *mod_args, ln_mix)
            hts.append(ht)
        yT = peer_pallas(jnp.concatenate(hts, axis=1), peer_wq[layer].T.astype(BF16),
                         peer_k1[layer].astype(BF16), peer_k2[layer].astype(BF16),
                         peer_u[layer].astype(BF16), peer_v[layer].T.astype(BF16))
        tile_off = 0
        for g in groups:
            g['x'] = peer_out_ln(yT, tile_off, g['x'], mods, g['per_mod'], g['mod_off'], ln_ffn)
            tile_off += g['x'].shape[0] // TOKEN_TM
    y_prompt = groups[0]['x'].reshape(BATCH, SEQ, D)
    y_sample = groups[1]['x'].reshape(DEC_BATCH, DEC_SEQ, D)
    return (y_prompt, y_sample, jnp.stack(new_f, axis=1), jnp.stack(new_b, axis=1))
```
